```python
import jax
import jax.numpy as jnp
from jax import lax

D_MODEL = 1024
BATCH = 4
SEQ = 4096
DEPTH = 2

N_A_LAYERS = DEPTH - DEPTH // 2
N_B_LAYERS = DEPTH // 2
N_DENSE = (DEPTH + 1) // 2
N_MOE = DEPTH // 2

HGRN_EXPAND = 128
HGRN_HEADS = D_MODEL // HGRN_EXPAND
HGRN_DV = D_MODEL // HGRN_HEADS
HGRN_CHUNK = 64

ATT_HEAD_DIM = 64
ATT_HEADS = D_MODEL // ATT_HEAD_DIM
DILATED_BRANCHES = ((128, 1), (512, 4), (2048, 16))
N_BRANCH = len(DILATED_BRANCHES)
BAND = DILATED_BRANCHES[0][0] // DILATED_BRANCHES[0][1]
ATT_SCALE = ATT_HEAD_DIM ** -0.5
ROPE_THETA = 10000.0

D_FF = 2816
N_EXPERTS = 8
TOP_K = 2
D_FF_EXPERT = 3584
EPS = 1e-6

kernel_name = "hgrn2_yoco_dilated_moe_trunk"


def rms_norm(x, g):
    xf = x.astype(jnp.float32)
    y = xf * lax.rsqrt(jnp.mean(xf * xf, axis=-1, keepdims=True) + EPS)
    return (y * g.astype(jnp.float32)).astype(x.dtype)


def rope(t, pos):
    half = t.shape[-1] // 2
    inv_freq = ROPE_THETA ** (-jnp.arange(half, dtype=jnp.float32) / half)
    ang = pos.astype(jnp.float32)[:, None] * inv_freq[None, :]
    cos = jnp.cos(ang)[None, :, None, :]
    sin = jnp.sin(ang)[None, :, None, :]
    tf = t.astype(jnp.float32)
    t1, t2 = tf[..., :half], tf[..., half:]
    return jnp.concatenate([t1 * cos - t2 * sin, t2 * cos + t1 * sin], axis=-1)


def swiglu(h, w_gate, w_up, w_down):
    return (jax.nn.silu(h @ w_gate) * (h @ w_up)) @ w_down


def hgrn2_mixer(h, w_in, lower_bound, out_norm, w_out):
    B, S, _ = h.shape
    H, DK, DV, C = HGRN_HEADS, HGRN_EXPAND, HGRN_DV, HGRN_CHUNK
    N = S // C
    q, f_logit, v, g = jnp.split(h @ w_in, 4, axis=-1)
    lb = lower_bound.astype(jnp.float32)
    f = lb + (1.0 - lb) * jax.nn.sigmoid(f_logit.astype(jnp.float32))
    k = 1.0 - f

    def chunks(t):
        return t.astype(jnp.float32).reshape(B, N, C, H, -1).transpose(0, 3, 1, 2, 4)

    q_c, k_c, v_c = chunks(q), chunks(k), chunks(v)
    b = jnp.cumsum(chunks(jnp.log(f)), axis=3)
    b_last = b[:, :, :, -1:, :]
    q_dec = q_c * jnp.exp(b)
    k_dec = k_c * jnp.exp(-b)
    k_end = k_c * jnp.exp(b_last - b)
    causal = jnp.tril(jnp.ones((C, C), dtype=bool))
    scores = jnp.where(causal, jnp.einsum("bhncd,bhnsd->bhncs", q_dec, k_dec), 0.0)
    o_intra = jnp.einsum("bhncs,bhnsv->bhncv", scores, v_c)
    kv_chunk = jnp.einsum("bhnsd,bhnsv->bhndv", k_end, v_c)
    decay = jnp.exp(b_last[:, :, :, 0, :])

    def step(state, inp):
        dcy, kv = inp
        return dcy[..., None] * state + kv, state

    state0 = jnp.zeros((B, H, DK, DV), jnp.float32)
    _, states = lax.scan(step, state0, (jnp.moveaxis(decay, 2, 0), jnp.moveaxis(kv_chunk, 2, 0)))
    o_inter = jnp.einsum("bhncd,nbhdv->bhncv", q_dec, states)
    o = (o_intra + o_inter).transpose(0, 2, 3, 1, 4).reshape(B, S, H, DV)
    o = rms_norm(o, out_norm) * jax.nn.silu(g.astype(jnp.float32)).reshape(B, S, H, DV)
    return o.reshape(B, S, D_MODEL).astype(h.dtype) @ w_out


def dilated_branch(q, k, v, dil):
    B, S, H, D = q.shape
    L = S // dil
    nb = -(-L // BAND)
    Lp = nb * BAND

    def to_blocks(t):
        t = t.reshape(B, L, dil, H, D).transpose(0, 2, 1, 3, 4)
        t = jnp.pad(t, ((0, 0), (0, 0), (0, Lp - L), (0, 0), (0, 0)))
        return t.reshape(B, dil, nb, BAND, H, D)

    def with_prev(t):
        prev = jnp.pad(t[:, :, :-1], ((0, 0), (0, 0), (1, 0), (0, 0), (0, 0), (0, 0)))
        return jnp.concatenate([prev, t], axis=3)

    qb = to_blocks(q)
    kb = with_prev(to_blocks(k))
    vb = with_prev(to_blocks(v))
    s = jnp.einsum("brnqhd,brnkhd->brnhqk", qb, kb)
    qi = jnp.arange(BAND)[:, None]
    ki = jnp.arange(2 * BAND)[None, :]
    dist = qi + BAND - ki
    band_ok = (dist >= 0) & (dist <= BAND)
    valid = band_ok[None] & ((jnp.arange(nb)[:, None, None] > 0) | (ki >= BAND)[None])
    s = jnp.where(valid[None, None, :, None], s, -jnp.inf)
    m = jnp.max(s, axis=-1)
    p = jnp.exp(s - m[..., None])
    l = jnp.sum(p, axis=-1)
    o = jnp.einsum("brnhqk,brnkhd->brnqhd", p, vb) / jnp.swapaxes(l, 3, 4)[..., None]
    lse = jnp.swapaxes(m + jnp.log(l), 3, 4)
    o = o.reshape(B, dil, Lp, H, D)[:, :, :L].transpose(0, 2, 1, 3, 4).reshape(B, S, H, D)
    lse = lse.reshape(B, dil, Lp, H)[:, :, :L].transpose(0, 2, 1, 3).reshape(B, S, H)
    return o, lse


def shared_kv(h, kv_norm, w_kv, pos):
    B, S, _ = h.shape
    kv = (rms_norm(h, kv_norm) @ w_kv).reshape(B, S, 2, N_BRANCH, ATT_HEADS, ATT_HEAD_DIM)
    keys = [rope(kv[:, :, 0, i], pos) for i in range(N_BRANCH)]
    values = [kv[:, :, 1, i].astype(jnp.float32) for i in range(N_BRANCH)]
    return keys, values


def dilated_mixer(h, w_q, w_out, keys, values, pos):
    B, S, _ = h.shape
    q = (h @ w_q).reshape(B, S, N_BRANCH, ATT_HEADS, ATT_HEAD_DIM)
    outs, lses = [], []
    for i, (_, dil) in enumerate(DILATED_BRANCHES):
        o_i, lse_i = dilated_branch(rope(q[:, :, i], pos) * ATT_SCALE, keys[i], values[i], dil)
        outs.append(o_i)
        lses.append(lse_i)
    wts = jax.nn.softmax(jnp.stack(lses, axis=0), axis=0)
    o = jnp.sum(wts[..., None] * jnp.stack(outs, axis=0), axis=0)
    return o.reshape(B, S, D_MODEL).astype(h.dtype) @ w_out


def moe_swiglu(h, w_router, w_gate, w_up, w_down):
    B, S, Dm = h.shape
    t = h.reshape(B * S, Dm)
    probs = jax.nn.softmax((t @ w_router).astype(jnp.float32), axis=-1)
    top_p, top_i = lax.top_k(probs, TOP_K)
    top_p = top_p / jnp.sum(top_p, axis=-1, keepdims=True)
    combine = jnp.sum(jax.nn.one_hot(top_i, N_EXPERTS, dtype=jnp.float32) * top_p[..., None], axis=1).astype(h.dtype)
    out = jnp.zeros_like(t)
    for e in range(N_EXPERTS):
        out = out + combine[:, e:e + 1] * swiglu(t, w_gate[e], w_up[e], w_down[e])
    return out.reshape(B, S, Dm)


def setup_inputs(seed: int = 0) -> dict:
    key = jax.random.key(seed)
    ks = jax.random.split(key, 20)
    f32 = jnp.float32

    def dense(k, shape, fan_in):
        return jax.random.normal(k, shape, f32) * fan_in ** -0.5

    def gain(k, shape):
        return 1.0 + 0.02 * jax.random.normal(k, shape, f32)

    att_w = ATT_HEADS * ATT_HEAD_DIM
    return {
        "x": jax.random.normal(ks[0], (BATCH, SEQ, D_MODEL), f32),
        "attn_norm": gain(ks[1], (DEPTH, D_MODEL)),
        "ffn_norm": gain(ks[2], (DEPTH, D_MODEL)),
        "hgrn_w_in": dense(ks[3], (N_A_LAYERS, D_MODEL, 4 * D_MODEL), D_MODEL),
        "hgrn_lower_bounds": 0.1 * jax.random.normal(ks[4], (N_A_LAYERS + 1, D_MODEL), f32),
        "hgrn_out_norm": gain(ks[5], (N_A_LAYERS, HGRN_DV)),
        "hgrn_w_out": dense(ks[6], (N_A_LAYERS, D_MODEL, D_MODEL), D_MODEL),
        "kv_norm": gain(ks[7], (D_MODEL,)),
        "w_kv": dense(ks[8], (D_MODEL, 2 * N_BRANCH * att_w), D_MODEL),
        "dil_w_q": dense(ks[9], (N_B_LAYERS, D_MODEL, N_BRANCH * att_w), D_MODEL),
        "dil_w_out": dense(ks[10], (N_B_LAYERS, att_w, D_MODEL), att_w),
        "ffn_w_gate": dense(ks[11], (N_DENSE, D_MODEL, D_FF), D_MODEL),
        "ffn_w_up": dense(ks[12], (N_DENSE, D_MODEL, D_FF), D_MODEL),
        "ffn_w_down": dense(ks[13], (N_DENSE, D_FF, D_MODEL), D_FF),
        "moe_w_router": dense(ks[14], (N_MOE, D_MODEL, N_EXPERTS), D_MODEL),
        "moe_w_gate": dense(ks[15], (N_MOE, N_EXPERTS, D_MODEL, D_FF_EXPERT), D_MODEL),
        "moe_w_up": dense(ks[16], (N_MOE, N_EXPERTS, D_MODEL, D_FF_EXPERT), D_MODEL),
        "moe_w_down": dense(ks[17], (N_MOE, N_EXPERTS, D_FF_EXPERT, D_MODEL), D_FF_EXPERT),
        "final_norm": gain(ks[18], (D_MODEL,)),
    }


def reference(x, attn_norm, ffn_norm, hgrn_w_in, hgrn_lower_bounds, hgrn_out_norm, hgrn_w_out,
              kv_norm, w_kv, dil_w_q, dil_w_out, ffn_w_gate, ffn_w_up, ffn_w_down,
              moe_w_router, moe_w_gate, moe_w_up, moe_w_down, final_norm):
    pos = jnp.arange(x.shape[1])
    lbs = jnp.cumsum(jax.nn.softmax(hgrn_lower_bounds.astype(jnp.float32), axis=0), axis=0)
    h = x
    keys, values = None, None
    for layer in range(DEPTH):
        if layer == N_A_LAYERS:
            keys, values = shared_kv(h, kv_norm, w_kv, pos)
        hn = rms_norm(h, attn_norm[layer])
        if layer < N_A_LAYERS:
            h = h + hgrn2_mixer(hn, hgrn_w_in[layer], lbs[layer], hgrn_out_norm[layer], hgrn_w_out[layer])
        else:
            j = layer - N_A_LAYERS
            h = h + dilated_mixer(hn, dil_w_q[j], dil_w_out[j], keys, values, pos)
        hn = rms_norm(h, ffn_norm[layer])
        if layer % 2 == 0:
            i = layer // 2
            h = h + swiglu(hn, ffn_w_gate[i], ffn_w_up[i], ffn_w_down[i])
        else:
            i = layer // 2
            h = h + moe_swiglu(hn, moe_w_router[i], moe_w_gate[i], moe_w_up[i], moe_w_down[i])
    return rms_norm(h, final_norm)
```

```python
import functools

import jax
import jax.numpy as jnp
from jax import lax
from jax.experimental import pallas as pl
from jax.experimental.pallas import tpu as pltpu

F32 = jnp.float32
BF16 = jnp.bfloat16

D_MODEL = 1024
HGRN_HEAD_DIM = 128
HGRN_HEADS = D_MODEL // HGRN_HEAD_DIM
HGRN_CHUNK = 64
ATT_HEAD_DIM = 64
ATT_HEADS = D_MODEL // ATT_HEAD_DIM
DILATIONS = (1, 4, 16)
N_BRANCH = len(DILATIONS)
BAND = 128
ATT_SCALE = ATT_HEAD_DIM ** -0.5
ROPE_THETA = 10000.0
N_EXPERTS = 8
TOP_K = 2
EPS = 1e-6

LANES = 128
TOKEN_TILE = 256
MOE_ROW_TILE = 256
VMEM_LIMIT = 56 * 1024 * 1024
NEG_BIG = -1e30


def _dot(a, b):
    return jnp.dot(a, b, preferred_element_type=F32)


def _dot_nt(a, b):
    return lax.dot_general(a, b, (((1,), (1,)), ((), ())), preferred_element_type=F32)


def _dot_tn(a, b):
    return lax.dot_general(a, b, (((0,), (0,)), ((), ())), preferred_element_type=F32)


def _rms(x, g):
    return x * lax.rsqrt(jnp.mean(x * x, axis=-1, keepdims=True) + EPS) * g


def _sigmoid(x):
    return 1.0 / (1.0 + jnp.exp(-x))


def _silu(x):
    return x * _sigmoid(x)


def _split3(x):
    hi = x.astype(BF16)
    r1 = x - hi.astype(F32)
    mid = r1.astype(BF16)
    lo = (r1 - mid.astype(F32)).astype(BF16)
    return hi, mid, lo


def _const_spec(shape):
    nd = len(shape)
    return pl.BlockSpec(shape, lambda *_: (0,) * nd, pipeline_mode=pl.Buffered(1))


def _params(sem):
    return pltpu.CompilerParams(dimension_semantics=sem, vmem_limit_bytes=VMEM_LIMIT)


def _hgrn_kernel(x_ref, g_ref, win_ref, lb_ref, onorm_ref, wout_ref, out_ref, state_ref, y_ref, o_ref):
    tm = x_ref.shape[1]
    dk = HGRN_HEAD_DIM
    c_len = HGRN_CHUNK

    @pl.when(pl.program_id(1) == 0)
    def _():
        state_ref[...] = jnp.zeros_like(state_ref)

    x = x_ref[0]
    xn = _rms(x, g_ref[...]).astype(BF16)
    y_ref[...] = _dot(xn, win_ref[...])

    r = lax.broadcasted_iota(jnp.int32, (tm, tm), 0)
    c = lax.broadcasted_iota(jnp.int32, (tm, tm), 1)
    tri = jnp.where((c <= r) & (c // c_len == r // c_len), 1.0, 0.0).astype(BF16)
    rc = lax.broadcasted_iota(jnp.int32, (c_len, c_len), 0)
    cc = lax.broadcasted_iota(jnp.int32, (c_len, c_len), 1)
    causal = cc <= rc

    for h in range(HGRN_HEADS):
        cols = slice(h * dk, (h + 1) * dk)
        q = y_ref[:, h * dk:(h + 1) * dk]
        f_logit = y_ref[:, D_MODEL + h * dk:D_MODEL + (h + 1) * dk]
        v = y_ref[:, 2 * D_MODEL + h * dk:2 * D_MODEL + (h + 1) * dk]
        gate = y_ref[:, 3 * D_MODEL + h * dk:3 * D_MODEL + (h + 1) * dk]
        lb = lb_ref[:, cols]
        f = lb + (1.0 - lb) * _sigmoid(f_logit)
        k = 1.0 - f
        lf_hi, lf_mid, lf_lo = _split3(jnp.log(f))
        b = _dot(tri, lf_hi) + _dot(tri, lf_mid) + _dot(tri, lf_lo)
        v16 = v.astype(BF16)
        for ci in range(tm // c_len):
            rows = slice(ci * c_len, (ci + 1) * c_len)
            bc = b[rows]
            b_last = bc[c_len - 1:c_len, :]
            q_dec = (q[rows] * jnp.exp(bc)).astype(BF16)
            k_dec = (k[rows] * jnp.exp(-bc)).astype(BF16)
            k_end = (k[rows] * jnp.exp(b_last - bc)).astype(BF16)
            scores = jnp.where(causal, _dot_nt(q_dec, k_dec), 0.0).astype(BF16)
            st = state_ref[h]
            o = _dot(scores, v16[rows]) + _dot_nt(q_dec, st.astype(BF16))
            state_ref[h] = st * jnp.exp(b_last) + _dot_tn(v16[rows], k_end)
            o = _rms(o, onorm_ref[...]) * _silu(gate[rows])
            o_ref[rows, cols] = o.astype(BF16)

    out_ref[0] = x + _dot(o_ref[...], wout_ref[...])


def _hgrn_mixer(x, g, w_in, lb, out_norm, w_out):
    B, S, D = x.shape
    tm = TOKEN_TILE
    return pl.pallas_call(
        _hgrn_kernel,
        grid=(B, S // tm),
        in_specs=[
            pl.BlockSpec((1, tm, D), lambda b, s: (b, s, 0)),
            _const_spec((1, D)),
            _const_spec((D, 4 * D)),
            _const_spec((1, D)),
            _const_spec((1, HGRN_HEAD_DIM)),
            _const_spec((D, D)),
        ],
        out_specs=pl.BlockSpec((1, tm, D), lambda b, s: (b, s, 0)),
        out_shape=jax.ShapeDtypeStruct((B, S, D), F32),
        scratch_shapes=[
            pltpu.VMEM((HGRN_HEADS, HGRN_HEAD_DIM, HGRN_HEAD_DIM), F32),
            pltpu.VMEM((tm, 4 * D), F32),
            pltpu.VMEM((tm, D), BF16),
        ],
        compiler_params=_params(("arbitrary", "arbitrary")),
        name="hgrn_mixer",
    )(x, g, w_in, lb, out_norm, w_out)


def _ffn_kernel(x_ref, g_ref, wg_ref, wu_ref, wd_ref, out_ref):
    x = x_ref[...]
    xn = _rms(x, g_ref[...]).astype(BF16)
    a = _silu(_dot(xn, wg_ref[...])) * _dot(xn, wu_ref[...])
    out_ref[...] = x + _dot(a.astype(BF16), wd_ref[...])


def _dense_ffn(x, g, w_gate, w_up, w_down):
    T, D = x.shape
    Fd = w_gate.shape[1]
    tm = TOKEN_TILE
    return pl.pallas_call(
        _ffn_kernel,
        grid=(T // tm,),
        in_specs=[
            pl.BlockSpec((tm, D), lambda i: (i, 0)),
            _const_spec((1, D)),
            _const_spec((D, Fd)),
            _const_spec((D, Fd)),
            _const_spec((Fd, D)),
        ],
        out_specs=pl.BlockSpec((tm, D), lambda i: (i, 0)),
        out_shape=jax.ShapeDtypeStruct((T, D), F32),
        compiler_params=_params(("arbitrary",)),
        name="dense_ffn",
    )(x, g, w_gate, w_up, w_down)


def _proj_kernel(x_ref, g_ref, w_ref, cos_ref, sin_ref, out_ref, *, rope):
    xn = _rms(x_ref[...], g_ref[...]).astype(BF16)
    n_out = w_ref.shape[1]
    for j in range(n_out // D_MODEL):
        t = _dot(xn, w_ref[:, j * D_MODEL:(j + 1) * D_MODEL])
        if rope:
            cos = cos_ref[...]
            sin = sin_ref[...]
            for p in range(D_MODEL // LANES):
                tp = t[:, p * LANES:(p + 1) * LANES]
                rot = pltpu.roll(tp, LANES // 2, 1)
                out_ref[:, j * D_MODEL + p * LANES:j * D_MODEL + (p + 1) * LANES] = (
                    tp * cos + rot * sin).astype(out_ref.dtype)
        else:
            out_ref[:, j * D_MODEL:(j + 1) * D_MODEL] = t.astype(out_ref.dtype)


def _norm_proj(x, g, w, cos, sin, seq_len, rope):
    T, D = x.shape
    N = w.shape[1]
    tm = TOKEN_TILE
    tiles_per_seq = seq_len // tm
    return pl.pallas_call(
        functools.partial(_proj_kernel, rope=rope),
        grid=(T // tm,),
        in_specs=[
            pl.BlockSpec((tm, D), lambda i: (i, 0)),
            _const_spec((1, D)),
            _const_spec((D, N)),
            pl.BlockSpec((tm, LANES), lambda i: (i % tiles_per_seq, 0)),
            pl.BlockSpec((tm, LANES), lambda i: (i % tiles_per_seq, 0)),
        ],
        out_specs=pl.BlockSpec((tm, N), lambda i: (i, 0)),
        out_shape=jax.ShapeDtypeStruct((T, N), BF16),
        compiler_params=_params(("arbitrary",)),
        name="norm_proj_rope" if rope else "norm_proj",
    )(x, g, w, cos, sin)


def _rope_tables(seq_len, scale):
    half = ATT_HEAD_DIM // 2
    inv_freq = ROPE_THETA ** (-jnp.arange(half, dtype=F32) / half)
    ang = jnp.arange(seq_len, dtype=F32)[:, None] * inv_freq[None, :]
    cos = jnp.tile(jnp.cos(ang), (1, LANES // half))
    sin = jnp.tile(jnp.sin(ang), (1, LANES // half))
    sign = jnp.where(jnp.arange(LANES) < LANES // 2, -1.0, 1.0).astype(F32)
    return cos * scale, sin * sign[None, :] * scale


def _pair_block_perm():
    half = ATT_HEAD_DIM // 2
    idx = []
    for p in range(ATT_HEADS // 2):
        base = p * LANES
        for part in range(2):
            for hh in range(2):
                start = base + hh * ATT_HEAD_DIM + part * half
                idx.extend(range(start, start + half))
    return jnp.asarray(idx, dtype=jnp.int32)


def _attn_kernel(q_ref, kp_ref, kc_ref, vp_ref, vc_ref, o_ref, lse_ref):
    n = pl.program_id(2)
    qi = lax.broadcasted_iota(jnp.int32, (BAND, 2 * BAND), 0)
    ki = lax.broadcasted_iota(jnp.int32, (BAND, 2 * BAND), 1)
    dist = qi + BAND - ki
    first_key = jnp.where(n > 0, 0, BAND)
    valid = (dist >= 0) & (dist <= BAND) & (ki >= first_key)
    lane = lax.broadcasted_iota(jnp.int32, (BAND, LANES), 1)
    lse_tile = jnp.zeros((BAND, LANES), F32)
    for p in range(ATT_HEADS // 2):
        cols = slice(p * LANES, (p + 1) * LANES)
        qp = q_ref[0, :, cols]
        kcat = jnp.concatenate([kp_ref[0, :, cols], kc_ref[0, :, cols]], axis=0)
        vcat = jnp.concatenate([vp_ref[0, :, cols], vc_ref[0, :, cols]], axis=0)
        o_pair = jnp.zeros((BAND, LANES), F32)
        for a in range(2):
            qm = jnp.where((lane // (ATT_HEAD_DIM // 2)) % 2 == a, qp, jnp.zeros_like(qp))
            s = jnp.where(valid, _dot_nt(qm, kcat), NEG_BIG)
            m = jnp.max(s, axis=-1, keepdims=True)
            pr = jnp.exp(s - m)
            l = jnp.sum(pr, axis=-1, keepdims=True)
            o = _dot(pr.astype(BF16), vcat) / l
            o_pair = jnp.where(lane // ATT_HEAD_DIM == a, o, o_pair)
            lse_tile = jnp.where(lane == 2 * p + a, m + jnp.log(l), lse_tile)
        o_ref[0, :, cols] = o_pair.astype(o_ref.dtype)
    lse_ref[0] = lse_tile


def _dilated_attn(q, kv, branch, dil):
    B, S, _ = q.shape
    D = D_MODEL
    L = S // dil
    nb = L // BAND
    qv = q.reshape(B, L, dil * N_BRANCH * D)
    kvv = kv.reshape(B, L, dil * 2 * N_BRANCH * D)
    blk = (1, BAND, D)

    def prev(n):
        return jnp.maximum(n - 1, 0)

    o, lse = pl.pallas_call(
        _attn_kernel,
        grid=(B, dil, nb),
        in_specs=[
            pl.BlockSpec(blk, lambda b, r, n: (b, n, r * N_BRANCH + branch)),
            pl.BlockSpec(blk, lambda b, r, n: (b, prev(n), r * 2 * N_BRANCH + branch)),
            pl.BlockSpec(blk, lambda b, r, n: (b, n, r * 2 * N_BRANCH + branch)),
            pl.BlockSpec(blk, lambda b, r, n: (b, prev(n), r * 2 * N_BRANCH + N_BRANCH + branch)),
            pl.BlockSpec(blk, lambda b, r, n: (b, n, r * 2 * N_BRANCH + N_BRANCH + branch)),
        ],
        out_specs=[
            pl.BlockSpec(blk, lambda b, r, n: (b, n, r)),
            pl.BlockSpec((1, BAND, LANES), lambda b, r, n: (b, n, r)),
        ],
        out_shape=[
            jax.ShapeDtypeStruct((B, L, dil * D), BF16),
            jax.ShapeDtypeStruct((B, L, dil * LANES), F32),
        ],
        compiler_params=_params(("arbitrary", "arbitrary", "arbitrary")),
        name=f"dilated_attn_d{dil}",
    )(qv, kvv, kvv, kvv, kvv)
    return o.reshape(B * S, D), lse.reshape(B * S, LANES)


def _combine_kernel(o0_ref, o1_ref, o2_ref, l0_ref, l1_ref, l2_ref, h_ref, wout_ref, expand_ref,
                    g_ref, wr_hi_ref, wr_lo_ref, h_out_ref, hn_out_ref, route_ref):
    lses = [l0_ref[...], l1_ref[...], l2_ref[...]]
    m = jnp.maximum(jnp.maximum(lses[0], lses[1]), lses[2])
    es = [jnp.exp(l - m) for l in lses]
    den = es[0] + es[1] + es[2]
    o_refs = [o0_ref, o1_ref, o2_ref]
    o = None
    for i in range(N_BRANCH):
        w = es[i] / den
        w_hi = w.astype(BF16)
        w_lo = (w - w_hi.astype(F32)).astype(BF16)
        w_full = _dot(w_hi, expand_ref[...]) + _dot(w_lo, expand_ref[...])
        term = w_full * o_refs[i][...].astype(F32)
        o = term if o is None else o + term
    h = h_ref[...] + _dot(o.astype(BF16), wout_ref[...])
    h_out_ref[...] = h
    hn = _rms(h, g_ref[...])
    hn_out_ref[...] = hn

    hn_hi = hn.astype(BF16)
    hn_lo = (hn - hn_hi.astype(F32)).astype(BF16)
    logits = _dot(hn_hi, wr_hi_ref[...]) + _dot(hn_hi, wr_lo_ref[...]) + _dot(hn_lo, wr_hi_ref[...])
    lane = lax.broadcasted_iota(jnp.int32, logits.shape, 1)
    lane_f = lane.astype(F32)
    logits = jnp.where(lane < N_EXPERTS, logits, NEG_BIG)
    mx = jnp.max(logits, axis=-1, keepdims=True)
    pe = jnp.exp(logits - mx)
    probs = pe / jnp.sum(pe, axis=-1, keepdims=True)
    probs = jnp.where(lane < N_EXPERTS, probs, -1.0)
    p1 = jnp.max(probs, axis=-1, keepdims=True)
    i1 = jnp.min(jnp.where(probs == p1, lane_f, float(LANES)), axis=-1, keepdims=True)
    rest = jnp.where(lane_f == i1, -1.0, probs)
    p2 = jnp.max(rest, axis=-1, keepdims=True)
    i2 = jnp.min(jnp.where(rest == p2, lane_f, float(LANES)), axis=-1, keepdims=True)
    tot = p1 + p2
    route = jnp.where(lane == 0, p1 / tot, 0.0)
    route = jnp.where(lane == 1, p2 / tot, route)
    route = jnp.where(lane == 2, i1, route)
    route = jnp.where(lane == 3, i2, route)
    route_ref[...] = route


def _attn_combine(os, lses, h, w_out, expand, g, wr_hi, wr_lo):
    T, D = h.shape
    tm = TOKEN_TILE
    row = lambda i: (i, 0)
    return pl.pallas_call(
        _combine_kernel,
        grid=(T // tm,),
        in_specs=[pl.BlockSpec((tm, D), row)] * 3 + [pl.BlockSpec((tm, LANES), row)] * 3 + [
            pl.BlockSpec((tm, D), row),
            _const_spec((D, D)),
            _const_spec((LANES, D)),
            _const_spec((1, D)),
            _const_spec((D, LANES)),
            _const_spec((D, LANES)),
        ],
        out_specs=[pl.BlockSpec((tm, D), row), pl.BlockSpec((tm, D), row), pl.BlockSpec((tm, LANES), row)],
        out_shape=[
            jax.ShapeDtypeStruct((T, D), F32),
            jax.ShapeDtypeStruct((T, D), F32),
            jax.ShapeDtypeStruct((T, LANES), F32),
        ],
        compiler_params=_params(("arbitrary",)),
        name="attn_combine_route",
    )(*os, *lses, h, w_out, expand, g, wr_hi, wr_lo)


def _row_gather_copy(src_hbm, row, dst_buf, slot, r, sem):
    return pltpu.make_async_copy(src_hbm.at[pl.ds(row, 1)], dst_buf.at[slot, pl.ds(r, 1)], sem.at[slot])


def _expert_kernel(te_ref, nt_ref, ids_ref, ids_next_ref, x_hbm, wg_ref, wu_ref, wd_ref, y_ref, xbuf, sem):
    del te_ref
    i = pl.program_id(0)
    n_tiles = nt_ref[0]
    tm = xbuf.shape[1]
    slot = i % 2

    def start_tile(ids, dst_slot):
        def body(r, carry):
            _row_gather_copy(x_hbm, ids[0, 0, r], xbuf, dst_slot, r, sem).start()
            return carry
        lax.fori_loop(0, tm, body, 0)

    def wait_tile(dst_slot):
        def body(r, carry):
            _row_gather_copy(x_hbm, 0, xbuf, dst_slot, r, sem).wait()
            return carry
        lax.fori_loop(0, tm, body, 0)

    @pl.when((i == 0) & (n_tiles > 0))
    def _():
        start_tile(ids_ref, 0)

    @pl.when(i + 1 < n_tiles)
    def _():
        start_tile(ids_next_ref, 1 - slot)

    @pl.when(i < n_tiles)
    def _():
        wait_tile(slot)
        x = xbuf[slot].astype(BF16)
        f_dim = wg_ref.shape[2]
        fc = 512
        acc = jnp.zeros((tm, D_MODEL), F32)
        for c in range(f_dim // fc):
            cs = slice(c * fc, (c + 1) * fc)
            a = _silu(_dot(x, wg_ref[0, :, cs])) * _dot(x, wu_ref[0, :, cs])
            acc = acc + _dot(a.astype(BF16), wd_ref[0, cs, :])
        y_ref[...] = acc

    @pl.when(i >= n_tiles)
    def _():
        y_ref[...] = jnp.zeros_like(y_ref)


def _moe_experts(hn, tile_expert, n_tiles, row_ids, w_gate, w_up, w_down):
    T, D = hn.shape
    n_exp, _, Fd = w_gate.shape
    tm = MOE_ROW_TILE
    G = row_ids.shape[0]
    grid_spec = pltpu.PrefetchScalarGridSpec(
        num_scalar_prefetch=2,
        grid=(G,),
        in_specs=[
            pl.BlockSpec((1, 1, tm), lambda i, te, nt: (i, 0, 0), memory_space=pltpu.SMEM),
            pl.BlockSpec((1, 1, tm), lambda i, te, nt: (jnp.minimum(i + 1, G - 1), 0, 0),
                         memory_space=pltpu.SMEM),
            pl.BlockSpec(memory_space=pl.ANY),
            pl.BlockSpec((1, D, Fd), lambda i, te, nt: (te[i], 0, 0), pipeline_mode=pl.Buffered(1)),
            pl.BlockSpec((1, D, Fd), lambda i, te, nt: (te[i], 0, 0), pipeline_mode=pl.Buffered(1)),
            pl.BlockSpec((1, Fd, D), lambda i, te, nt: (te[i], 0, 0), pipeline_mode=pl.Buffered(1)),
        ],
        out_specs=pl.BlockSpec((tm, D), lambda i, te, nt: (i, 0)),
        scratch_shapes=[pltpu.VMEM((2, tm, D), F32), pltpu.SemaphoreType.DMA((2,))],
    )
    return pl.pallas_call(
        _expert_kernel,
        grid_spec=grid_spec,
        out_shape=jax.ShapeDtypeStruct((G * tm, D), F32),
        compiler_params=_params(("arbitrary",)),
        name="moe_experts",
    )(tile_expert, n_tiles, row_ids, row_ids, hn, w_gate, w_up, w_down)


def _finish_kernel(pos_ref, pos_next_ref, y_hbm, h_ref, route_ref, g_ref, out_ref, ybuf, sem):
    i = pl.program_id(0)
    n = pl.num_programs(0)
    tm = h_ref.shape[0]
    slot = i % 2

    def start_tile(pos, dst_slot):
        def body(r, carry):
            _row_gather_copy(y_hbm, pos[0, 0, r], ybuf, dst_slot, r, sem).start()
            return carry
        lax.fori_loop(0, TOP_K * tm, body, 0)

    def wait_tile(dst_slot):
        def body(r, carry):
            _row_gather_copy(y_hbm, 0, ybuf, dst_slot, r, sem).wait()
            return carry
        lax.fori_loop(0, TOP_K * tm, body, 0)

    @pl.when(i == 0)
    def _():
        start_tile(pos_ref, 0)

    @pl.when(i + 1 < n)
    def _():
        start_tile(pos_next_ref, 1 - slot)

    wait_tile(slot)
    route = route_ref[...]
    w0 = route[:, 0:1]
    w1 = route[:, 1:2]
    h = h_ref[...] + w0 * ybuf[slot, 0:tm, :] + w1 * ybuf[slot, tm:2 * tm, :]
    out_ref[...] = _rms(h, g_ref[...])


def _moe_finish(y_sorted, pos, h, route, g):
    T, D = h.shape
    tm = TOKEN_TILE
    n = T // tm
    return pl.pallas_call(
        _finish_kernel,
        grid=(n,),
        in_specs=[
            pl.BlockSpec((1, 1, TOP_K * tm), lambda i: (i, 0, 0), memory_space=pltpu.SMEM),
            pl.BlockSpec((1, 1, TOP_K * tm), lambda i: (jnp.minimum(i + 1, n - 1), 0, 0),
                         memory_space=pltpu.SMEM),
            pl.BlockSpec(memory_space=pl.ANY),
            pl.BlockSpec((tm, D), lambda i: (i, 0)),
            pl.BlockSpec((tm, LANES), lambda i: (i, 0)),
            _const_spec((1, D)),
        ],
        out_specs=pl.BlockSpec((tm, D), lambda i: (i, 0)),
        out_shape=jax.ShapeDtypeStruct((T, D), F32),
        scratch_shapes=[pltpu.VMEM((2, TOP_K * tm, D), F32), pltpu.SemaphoreType.DMA((2,))],
        compiler_params=_params(("arbitrary",)),
        name="moe_finish",
    )(pos, pos, y_sorted, h, route, g)


def _route_metadata(route, tm, tok_tile):
    T = route.shape[0]
    flat_e = route[:, 2:2 + TOP_K].astype(jnp.int32).reshape(-1)
    onehot = (flat_e[:, None] == jnp.arange(N_EXPERTS, dtype=jnp.int32)[None, :]).astype(jnp.int32)
    csum = jnp.cumsum(onehot, axis=0)
    counts = csum[-1]
    rank = jnp.sum(onehot * csum, axis=1) - 1
    padded = ((counts + tm - 1) // tm) * tm
    ends = jnp.cumsum(padded)
    starts = ends - padded
    pos = jnp.sum(onehot * starts[None, :], axis=1) + rank
    G = (TOP_K * T) // tm + N_EXPERTS
    row_token = jnp.zeros((G * tm,), jnp.int32).at[pos].set(jnp.arange(TOP_K * T, dtype=jnp.int32) // TOP_K)
    tile_start = jnp.arange(G, dtype=jnp.int32) * tm
    tile_expert = jnp.minimum(jnp.sum((tile_start[:, None] >= ends[None, :]).astype(jnp.int32), axis=1),
                              N_EXPERTS - 1)
    n_tiles = (ends[-1] // tm).reshape(1)
    pos_tiles = pos.reshape(T // tok_tile, tok_tile, TOP_K).transpose(0, 2, 1).reshape(T // tok_tile, 1,
                                                                                       TOP_K * tok_tile)
    return tile_expert, n_tiles, row_token.reshape(G, 1, tm), pos_tiles


def kernel(x, attn_norm, ffn_norm, hgrn_w_in, hgrn_lower_bounds, hgrn_out_norm, hgrn_w_out, kv_norm, w_kv,
           dil_w_q, dil_w_out, ffn_w_gate, ffn_w_up, ffn_w_down, moe_w_router, moe_w_gate, moe_w_up,
           moe_w_down, final_norm):
    B, S, D = x.shape
    T = B * S
    row = lambda v: v.reshape(1, -1).astype(F32)

    lbs = jnp.cumsum(jax.nn.softmax(hgrn_lower_bounds.astype(F32), axis=0), axis=0)
    h = _hgrn_mixer(x, row(attn_norm[0]), hgrn_w_in[0].astype(BF16), row(lbs[0]), row(hgrn_out_norm[0]),
                    hgrn_w_out[0].astype(BF16))
    h = h.reshape(T, D)
    h = _dense_ffn(h, row(ffn_norm[0]), ffn_w_gate[0].astype(BF16), ffn_w_up[0].astype(BF16),
                   ffn_w_down[0].astype(BF16))

    perm = _pair_block_perm()
    perm3 = jnp.concatenate([perm + i * D for i in range(N_BRANCH)])
    cos_k, sin_k = _rope_tables(S, 1.0)
    cos_q, sin_q = _rope_tables(S, ATT_SCALE)
    w_k = w_kv[:, :N_BRANCH * D][:, perm3].astype(BF16)
    w_v = w_kv[:, N_BRANCH * D:].astype(BF16)
    w_q = dil_w_q[0][:, perm3].astype(BF16)
    k_all = _norm_proj(h, row(kv_norm), w_k, cos_k, sin_k, S, rope=True)
    v_all = _norm_proj(h, row(kv_norm), w_v, cos_k, sin_k, S, rope=False)
    q_all = _norm_proj(h, row(attn_norm[1]), w_q, cos_q, sin_q, S, rope=True)
    kv_all = jnp.concatenate([k_all, v_all], axis=1).reshape(B, S, 2 * N_BRANCH * D)
    q_all = q_all.reshape(B, S, N_BRANCH * D)
    os, lses = [], []
    for i, dil in enumerate(DILATIONS):
        o_i, lse_i = _dilated_attn(q_all, kv_all, i, dil)
        os.append(o_i)
        lses.append(lse_i)

    head_of_col = jnp.arange(D, dtype=jnp.int32) // ATT_HEAD_DIM
    expand = (jnp.arange(LANES, dtype=jnp.int32)[:, None] == head_of_col[None, :]).astype(BF16)
    wr = jnp.zeros((D, LANES), F32).at[:, :N_EXPERTS].set(moe_w_router[0].astype(F32))
    wr_hi = wr.astype(BF16)
    wr_lo = (wr - wr_hi.astype(F32)).astype(BF16)
    h, hn, route = _attn_combine(os, lses, h, dil_w_out[0].astype(BF16), expand, row(ffn_norm[1]), wr_hi, wr_lo)

    tile_expert, n_tiles, row_ids, pos_tiles = _route_metadata(route, MOE_ROW_TILE, TOKEN_TILE)
    y_sorted = _moe_experts(hn, tile_expert, n_tiles, row_ids, moe_w_gate[0].astype(BF16),
                            moe_w_up[0].astype(BF16), moe_w_down[0].astype(BF16))
    out = _moe_finish(y_sorted, pos_tiles, h, route, row(final_norm))
    return out.reshape(B, S, D)
```

```python
import functools

import jax
import jax.numpy as jnp
from jax import lax
from jax.experimental import pallas as pl
from jax.experimental.pallas import tpu as pltpu

F32 = jnp.float32
BF16 = jnp.bfloat16

D_MODEL = 1024
HGRN_HEAD_DIM = 128
HGRN_HEADS = D_MODEL // HGRN_HEAD_DIM
HGRN_CHUNK = 64
ATT_HEAD_DIM = 64
ATT_HEADS = D_MODEL // ATT_HEAD_DIM
DILATIONS = (1, 4, 16)
N_BRANCH = len(DILATIONS)
BAND = 128
ATT_SCALE = ATT_HEAD_DIM ** -0.5
ROPE_THETA = 10000.0
N_EXPERTS = 8
TOP_K = 2
EPS = 1e-6

LANES = 128
SUBLANES = 8
TOKEN_TILE = 256
MOE_ROW_TILE = TOKEN_TILE
VMEM_LIMIT = 56 * 1024 * 1024
NEG_BIG = -1e30


def _dot(a, b):
    return jnp.dot(a, b, preferred_element_type=F32)


def _dot_nt(a, b):
    return lax.dot_general(a, b, (((1,), (1,)), ((), ())), preferred_element_type=F32)


def _dot_tn(a, b):
    return lax.dot_general(a, b, (((0,), (0,)), ((), ())), preferred_element_type=F32)


def _rms(x, g):
    return x * lax.rsqrt(jnp.mean(x * x, axis=-1, keepdims=True) + EPS) * g


def _sigmoid(x):
    return 1.0 / (1.0 + jnp.exp(-x))


def _silu(x):
    return x * _sigmoid(x)


def _split3(x):
    hi = x.astype(BF16)
    r1 = x - hi.astype(F32)
    mid = r1.astype(BF16)
    lo = (r1 - mid.astype(F32)).astype(BF16)
    return hi, mid, lo


def _const_spec(shape):
    nd = len(shape)
    return pl.BlockSpec(shape, lambda *_: (0,) * nd, pipeline_mode=pl.Buffered(1))


def _params(sem):
    return pltpu.CompilerParams(dimension_semantics=sem, vmem_limit_bytes=VMEM_LIMIT)


def _hgrn_kernel(x_ref, g_ref, win_ref, lb_ref, onorm_ref, wout_ref, out_ref, state_ref, y_ref, o_ref):
    tm = x_ref.shape[1]
    dk = HGRN_HEAD_DIM
    c_len = HGRN_CHUNK

    @pl.when(pl.program_id(1) == 0)
    def _():
        state_ref[...] = jnp.zeros_like(state_ref)

    x = x_ref[0]
    xn = _rms(x, g_ref[...]).astype(BF16)
    y_ref[...] = _dot(xn, win_ref[...])

    r = lax.broadcasted_iota(jnp.int32, (tm, tm), 0)
    c = lax.broadcasted_iota(jnp.int32, (tm, tm), 1)
    tri = jnp.where((c <= r) & (c // c_len == r // c_len), 1.0, 0.0).astype(BF16)
    rc = lax.broadcasted_iota(jnp.int32, (c_len, c_len), 0)
    cc = lax.broadcasted_iota(jnp.int32, (c_len, c_len), 1)
    causal = cc <= rc

    for h in range(HGRN_HEADS):
        cols = slice(h * dk, (h + 1) * dk)
        q = y_ref[:, h * dk:(h + 1) * dk]
        f_logit = y_ref[:, D_MODEL + h * dk:D_MODEL + (h + 1) * dk]
        v = y_ref[:, 2 * D_MODEL + h * dk:2 * D_MODEL + (h + 1) * dk]
        gate = y_ref[:, 3 * D_MODEL + h * dk:3 * D_MODEL + (h + 1) * dk]
        lb = lb_ref[:, cols]
        f = lb + (1.0 - lb) * _sigmoid(f_logit)
        k = 1.0 - f
        lf_hi, lf_mid, lf_lo = _split3(jnp.log(f))
        b = _dot(tri, lf_hi) + _dot(tri, lf_mid) + _dot(tri, lf_lo)
        v16 = v.astype(BF16)
        for ci in range(tm // c_len):
            rows = slice(ci * c_len, (ci + 1) * c_len)
            bc = b[rows]
            b_last = bc[c_len - 1:c_len, :]
            q_dec = (q[rows] * jnp.exp(bc)).astype(BF16)
            k_dec = (k[rows] * jnp.exp(-bc)).astype(BF16)
            k_end = (k[rows] * jnp.exp(b_last - bc)).astype(BF16)
            scores = jnp.where(causal, _dot_nt(q_dec, k_dec), 0.0).astype(BF16)
            st = state_ref[h]
            o = _dot(scores, v16[rows]) + _dot_nt(q_dec, st.astype(BF16))
            state_ref[h] = st * jnp.exp(b_last) + _dot_tn(v16[rows], k_end)
            o = _rms(o, onorm_ref[...]) * _silu(gate[rows])
            o_ref[rows, cols] = o.astype(BF16)

    out_ref[0] = x + _dot(o_ref[...], wout_ref[...])


def _hgrn_mixer(x, g, w_in, lb, out_norm, w_out):
    B, S, D = x.shape
    tm = TOKEN_TILE
    return pl.pallas_call(
        _hgrn_kernel,
        grid=(B, S // tm),
        in_specs=[
            pl.BlockSpec((1, tm, D), lambda b, s: (b, s, 0)),
            _const_spec((1, D)),
            _const_spec((D, 4 * D)),
            _const_spec((1, D)),
            _const_spec((1, HGRN_HEAD_DIM)),
            _const_spec((D, D)),
        ],
        out_specs=pl.BlockSpec((1, tm, D), lambda b, s: (b, s, 0)),
        out_shape=jax.ShapeDtypeStruct((B, S, D), F32),
        scratch_shapes=[
            pltpu.VMEM((HGRN_HEADS, HGRN_HEAD_DIM, HGRN_HEAD_DIM), F32),
            pltpu.VMEM((tm, 4 * D), F32),
            pltpu.VMEM((tm, D), BF16),
        ],
        compiler_params=_params(("arbitrary", "arbitrary")),
        name="hgrn_mixer",
    )(x, g, w_in, lb, out_norm, w_out)


def _ffn_kernel(x_ref, g_ref, wg_ref, wu_ref, wd_ref, out_ref):
    x = x_ref[0]
    xn = _rms(x, g_ref[...]).astype(BF16)
    a = _silu(_dot(xn, wg_ref[...])) * _dot(xn, wu_ref[...])
    out_ref[0] = x + _dot(a.astype(BF16), wd_ref[...])


def _dense_ffn(x, g, w_gate, w_up, w_down):
    B, S, D = x.shape
    Fd = w_gate.shape[1]
    tm = TOKEN_TILE
    return pl.pallas_call(
        _ffn_kernel,
        grid=(B, S // tm),
        in_specs=[
            pl.BlockSpec((1, tm, D), lambda b, s: (b, s, 0)),
            _const_spec((1, D)),
            _const_spec((D, Fd)),
            _const_spec((D, Fd)),
            _const_spec((Fd, D)),
        ],
        out_specs=pl.BlockSpec((1, tm, D), lambda b, s: (b, s, 0)),
        out_shape=jax.ShapeDtypeStruct((B, S, D), F32),
        compiler_params=_params(("arbitrary", "arbitrary")),
        name="dense_ffn",
    )(x, g, w_gate, w_up, w_down)


def _proj_kernel(x_ref, g_ref, w_ref, cos_ref, sin_ref, o0_ref, o1_ref, o2_ref, t_ref, *, rope):
    tm = x_ref.shape[1]
    xn = _rms(x_ref[0], g_ref[...]).astype(BF16)
    out_refs = (o0_ref, o1_ref, o2_ref)
    for j, dil in enumerate(DILATIONS):
        t = _dot(xn, w_ref[:, j * D_MODEL:(j + 1) * D_MODEL])
        for p in range(D_MODEL // LANES):
            cols = slice(p * LANES, (p + 1) * LANES)
            tp = t[:, cols]
            if rope:
                tp = tp * cos_ref[...] + pltpu.roll(tp, LANES // 2, 1) * sin_ref[...]
            if dil == 1:
                out_refs[j][0, 0, :, cols] = tp.astype(BF16)
            else:
                t_ref[p] = tp
                for r in range(dil):
                    out_refs[j][0, r, :, cols] = t_ref[p, pl.ds(r, tm // dil, stride=dil), :].astype(BF16)


def _norm_proj(x, g, w, cos, sin, rope):
    B, S, D = x.shape
    tm = TOKEN_TILE
    return pl.pallas_call(
        functools.partial(_proj_kernel, rope=rope),
        grid=(B, S // tm),
        in_specs=[
            pl.BlockSpec((1, tm, D), lambda b, s: (b, s, 0)),
            _const_spec((1, D)),
            _const_spec((D, N_BRANCH * D)),
            pl.BlockSpec((tm, LANES), lambda b, s: (s, 0)),
            pl.BlockSpec((tm, LANES), lambda b, s: (s, 0)),
        ],
        out_specs=[pl.BlockSpec((1, dil, tm // dil, D), lambda b, s: (b, 0, s, 0)) for dil in DILATIONS],
        out_shape=[jax.ShapeDtypeStruct((B, dil, S // dil, D), BF16) for dil in DILATIONS],
        scratch_shapes=[pltpu.VMEM((D // LANES, tm, LANES), F32)],
        compiler_params=_params(("arbitrary", "arbitrary")),
        name="norm_proj_rope" if rope else "norm_proj",
    )(x, g, w, cos, sin)


def _rope_tables(seq_len, scale):
    half = ATT_HEAD_DIM // 2
    inv_freq = ROPE_THETA ** (-jnp.arange(half, dtype=F32) / half)
    ang = jnp.arange(seq_len, dtype=F32)[:, None] * inv_freq[None, :]
    cos = jnp.tile(jnp.cos(ang), (1, LANES // half))
    sin = jnp.tile(jnp.sin(ang), (1, LANES // half))
    sign = jnp.where(jnp.arange(LANES) < LANES // 2, -1.0, 1.0).astype(F32)
    return cos * scale, sin * sign[None, :] * scale


def _pair_block_perm():
    half = ATT_HEAD_DIM // 2
    idx = []
    for p in range(ATT_HEADS // 2):
        base = p * LANES
        for part in range(2):
            for hh in range(2):
                start = base + hh * ATT_HEAD_DIM + part * half
                idx.extend(range(start, start + half))
    return jnp.asarray(idx, dtype=jnp.int32)


def _attn_kernel(q_ref, kp_ref, kc_ref, vp_ref, vc_ref, o_ref, lse_ref):
    n = pl.program_id(2)
    qi = lax.broadcasted_iota(jnp.int32, (BAND, 2 * BAND), 0)
    ki = lax.broadcasted_iota(jnp.int32, (BAND, 2 * BAND), 1)
    dist = qi + BAND - ki
    first_key = jnp.where(n > 0, 0, BAND)
    valid = (dist >= 0) & (dist <= BAND) & (ki >= first_key)
    lane = lax.broadcasted_iota(jnp.int32, (BAND, LANES), 1)
    lse_tile = jnp.zeros((BAND, LANES), F32)
    for p in range(ATT_HEADS // 2):
        cols = slice(p * LANES, (p + 1) * LANES)
        qp = q_ref[:, cols]
        kcat = jnp.concatenate([kp_ref[:, cols], kc_ref[:, cols]], axis=0)
        vcat = jnp.concatenate([vp_ref[:, cols], vc_ref[:, cols]], axis=0)
        o_pair = jnp.zeros((BAND, LANES), F32)
        for a in range(2):
            qm = jnp.where((lane // (ATT_HEAD_DIM // 2)) % 2 == a, qp, jnp.zeros_like(qp))
            s = jnp.where(valid, _dot_nt(qm, kcat), NEG_BIG)
            m = jnp.max(s, axis=-1, keepdims=True)
            pr = jnp.exp(s - m)
            l = jnp.sum(pr, axis=-1, keepdims=True)
            o = _dot(pr.astype(BF16), vcat) / l
            o_pair = jnp.where(lane // ATT_HEAD_DIM == a, o, o_pair)
            lse_tile = jnp.where(lane == 2 * p + a, m + jnp.log(l), lse_tile)
        o_ref[:, cols] = o_pair.astype(o_ref.dtype)
    lse_ref[...] = lse_tile


def _dilated_attn(q, k, v, dil):
    B, _, L, D = q.shape
    nb = L // BAND
    blk = (None, None, BAND, D)
    cur = lambda b, r, n: (b, r, n, 0)
    prev = lambda b, r, n: (b, r, jnp.maximum(n - 1, 0), 0)
    return pl.pallas_call(
        _attn_kernel,
        grid=(B, dil, nb),
        in_specs=[
            pl.BlockSpec(blk, cur),
            pl.BlockSpec(blk, prev),
            pl.BlockSpec(blk, cur),
            pl.BlockSpec(blk, prev),
            pl.BlockSpec(blk, cur),
        ],
        out_specs=[pl.BlockSpec(blk, cur), pl.BlockSpec((None, None, BAND, LANES), cur)],
        out_shape=[
            jax.ShapeDtypeStruct((B, dil, L, D), BF16),
            jax.ShapeDtypeStruct((B, dil, L, LANES), F32),
        ],
        compiler_params=_params(("arbitrary", "arbitrary", "arbitrary")),
        name=f"dilated_attn_d{dil}",
    )(q, k, k, v, v)


def _combine_kernel(o0_ref, o1_ref, o2_ref, l0_ref, l1_ref, l2_ref, h_ref, wout_ref, expand_ref,
                    g_ref, wr_hi_ref, wr_lo_ref, h_out_ref, hn_out_ref, route_ref, o_scr, l_scr):
    tm = h_ref.shape[1]
    o_refs = (o0_ref, o1_ref, o2_ref)
    l_refs = (l0_ref, l1_ref, l2_ref)
    for i, dil in enumerate(DILATIONS):
        for r in range(dil):
            rows = pl.ds(r, tm // dil, stride=dil)
            l_scr[i, rows, :] = l_refs[i][0, r]
            for p in range(D_MODEL // LANES):
                o_scr[i, p, rows, :] = o_refs[i][0, r, :, p * LANES:(p + 1) * LANES].astype(F32)
    lses = [l_scr[i] for i in range(N_BRANCH)]
    m = jnp.maximum(jnp.maximum(lses[0], lses[1]), lses[2])
    es = [jnp.exp(l - m) for l in lses]
    den = es[0] + es[1] + es[2]
    o = None
    for i in range(N_BRANCH):
        w = es[i] / den
        w_hi = w.astype(BF16)
        w_lo = (w - w_hi.astype(F32)).astype(BF16)
        w_full = _dot(w_hi, expand_ref[...]) + _dot(w_lo, expand_ref[...])
        term = w_full * jnp.concatenate([o_scr[i, p] for p in range(D_MODEL // LANES)], axis=1)
        o = term if o is None else o + term
    h = h_ref[0] + _dot(o.astype(BF16), wout_ref[...])
    h_out_ref[...] = h
    hn = _rms(h, g_ref[...])
    hn_hi = hn.astype(BF16)
    hn_out_ref[...] = hn_hi

    hn_lo = (hn - hn_hi.astype(F32)).astype(BF16)
    logits = _dot(hn_hi, wr_hi_ref[...]) + _dot(hn_hi, wr_lo_ref[...]) + _dot(hn_lo, wr_hi_ref[...])
    lane = lax.broadcasted_iota(jnp.int32, logits.shape, 1)
    lane_f = lane.astype(F32)
    logits = jnp.where(lane < N_EXPERTS, logits, NEG_BIG)
    mx = jnp.max(logits, axis=-1, keepdims=True)
    pe = jnp.exp(logits - mx)
    probs = pe / jnp.sum(pe, axis=-1, keepdims=True)
    probs = jnp.where(lane < N_EXPERTS, probs, -1.0)
    p1 = jnp.max(probs, axis=-1, keepdims=True)
    i1 = jnp.min(jnp.where(probs == p1, lane_f, float(LANES)), axis=-1, keepdims=True)
    rest = jnp.where(lane_f == i1, -1.0, probs)
    p2 = jnp.max(rest, axis=-1, keepdims=True)
    i2 = jnp.min(jnp.where(rest == p2, lane_f, float(LANES)), axis=-1, keepdims=True)
    tot = p1 + p2
    route = jnp.where(lane_f == i1, p1 / tot, 0.0)
    route = jnp.where(lane_f == i2, p2 / tot, route)
    sel = jnp.where((lane_f == i1 + N_EXPERTS) | (lane_f == i2 + N_EXPERTS), 1.0, 0.0)
    route_ref[...] = route + sel


def _attn_combine(os, lses, h, w_out, expand, g, wr_hi, wr_lo):
    B, S, D = h.shape
    T = B * S
    tm = TOKEN_TILE
    ts = S // tm
    row = lambda b, s: (b * ts + s, 0)
    o_specs = [pl.BlockSpec((1, dil, tm // dil, D), lambda b, s: (b, 0, s, 0)) for dil in DILATIONS]
    l_specs = [pl.BlockSpec((1, dil, tm // dil, LANES), lambda b, s: (b, 0, s, 0)) for dil in DILATIONS]
    return pl.pallas_call(
        _combine_kernel,
        grid=(B, ts),
        in_specs=o_specs + l_specs + [
            pl.BlockSpec((1, tm, D), lambda b, s: (b, s, 0)),
            _const_spec((D, D)),
            _const_spec((LANES, D)),
            _const_spec((1, D)),
            _const_spec((D, LANES)),
            _const_spec((D, LANES)),
        ],
        out_specs=[pl.BlockSpec((tm, D), row), pl.BlockSpec((tm, D), row), pl.BlockSpec((tm, LANES), row)],
        out_shape=[
            jax.ShapeDtypeStruct((T, D), F32),
            jax.ShapeDtypeStruct((T, D), BF16),
            jax.ShapeDtypeStruct((T, LANES), F32),
        ],
        scratch_shapes=[pltpu.VMEM((N_BRANCH, D // LANES, tm, LANES), F32),
                        pltpu.VMEM((N_BRANCH, tm, LANES), F32)],
        compiler_params=_params(("arbitrary", "arbitrary")),
        name="attn_combine_route",
    )(*os, *lses, h, w_out, expand, g, wr_hi, wr_lo)


def _slot_one_hot(route):
    tm = route.shape[0]
    lane = lax.broadcasted_iota(jnp.int32, route.shape, 1)
    sel = jnp.where((lane >= N_EXPERTS) & (lane < 2 * N_EXPERTS), route, 0.0)
    r = lax.broadcasted_iota(jnp.int32, (tm, tm), 0)
    c = lax.broadcasted_iota(jnp.int32, (tm, tm), 1)
    strict_lower = jnp.where(c < r, 1.0, 0.0).astype(BF16)
    rank = _dot(strict_lower, sel.astype(BF16))
    slot_iota = lax.broadcasted_iota(jnp.int32, (tm, tm), 1).astype(F32)
    out = []
    for e in range(N_EXPERTS):
        col = N_EXPERTS + e
        hit = (slot_iota == rank[:, col:col + 1]) & (sel[:, col:col + 1] > 0.5)
        out.append(jnp.where(hit, 1.0, 0.0).astype(BF16))
    return out


def _block_copy(src, dst, sem):
    return pltpu.make_async_copy(src, dst, sem)


def _dispatch_kernel(offs_ref, fill_ref, route_ref, hn_ref, xs_hbm, stage, sem):
    j = pl.program_id(0)
    n = pl.num_programs(0)
    tm = hn_ref.shape[0]
    n_tiles = xs_hbm.shape[0] // tm
    slot = j % 2

    def zero_copy(row):
        return _block_copy(stage.at[1, 0], xs_hbm.at[pl.ds(pl.multiple_of(row, SUBLANES), tm)], sem.at[1])

    @pl.when(j == 0)
    def _():
        stage[1, 0] = jnp.zeros((tm, D_MODEL), F32)
        first_tail = fill_ref[N_EXPERTS]

        def start_tail(t, carry):
            zero_copy(t * tm).start()
            return carry

        def wait_tail(t, carry):
            zero_copy(t * tm).wait()
            return carry

        first = [zero_copy(fill_ref[e]) for e in range(N_EXPERTS)]
        second = [zero_copy(fill_ref[e] + tm) for e in range(N_EXPERTS)]
        for cp in first:
            cp.start()
        lax.fori_loop(first_tail, n_tiles, start_tail, 0)
        for cp in first:
            cp.wait()
        lax.fori_loop(first_tail, n_tiles, wait_tail, 0)
        for cp in second:
            cp.start()
        for cp in second:
            cp.wait()

    hn = hn_ref[...]
    one_hots = _slot_one_hot(route_ref[...])
    for e in range(N_EXPERTS):
        stage[slot, e] = _dot_tn(one_hots[e], hn)

    def copies(step_slot, step):
        return [_block_copy(stage.at[step_slot, e],
                            xs_hbm.at[pl.ds(pl.multiple_of(offs_ref[step * N_EXPERTS + e], SUBLANES), tm)],
                            sem.at[step_slot])
                for e in range(N_EXPERTS)]

    @pl.when(j > 0)
    def _():
        for cp in copies(1 - slot, j - 1):
            cp.wait()

    for cp in copies(slot, j):
        cp.start()

    @pl.when(j == n - 1)
    def _():
        for cp in copies(slot, j):
            cp.wait()


def _moe_dispatch(route, hn, offs, fill, n_rows):
    T, D = hn.shape
    tm = TOKEN_TILE
    grid_spec = pltpu.PrefetchScalarGridSpec(
        num_scalar_prefetch=2,
        grid=(T // tm,),
        in_specs=[
            pl.BlockSpec((tm, LANES), lambda j, o, f: (j, 0)),
            pl.BlockSpec((tm, D), lambda j, o, f: (j, 0)),
        ],
        out_specs=pl.BlockSpec(memory_space=pl.ANY),
        scratch_shapes=[pltpu.VMEM((2, N_EXPERTS, tm, D), F32), pltpu.SemaphoreType.DMA((2,))],
    )
    return pl.pallas_call(
        _dispatch_kernel,
        grid_spec=grid_spec,
        out_shape=jax.ShapeDtypeStruct((n_rows, D), F32),
        compiler_params=_params(("arbitrary",)),
        name="moe_dispatch",
    )(offs, fill, route, hn)


def _expert_kernel(te_ref, xb_ref, act_ref, x_ref, wg_ref, wu_ref, wd_ref, y_ref):
    del te_ref, xb_ref
    i = pl.program_id(0)
    tm = x_ref.shape[0]

    @pl.when(act_ref[i] > 0)
    def _():
        x = x_ref[...].astype(BF16)
        f_dim = wg_ref.shape[2]
        fc = 512
        acc = jnp.zeros((tm, D_MODEL), F32)
        for c in range(f_dim // fc):
            cs = slice(c * fc, (c + 1) * fc)
            a = _silu(_dot(x, wg_ref[0, :, cs])) * _dot(x, wu_ref[0, :, cs])
            acc = acc + _dot(a.astype(BF16), wd_ref[0, cs, :])
        y_ref[...] = acc

    @pl.when(act_ref[i] == 0)
    def _():
        y_ref[...] = jnp.zeros_like(y_ref)


def _moe_experts(xs, tile_expert, x_block, active, w_gate, w_up, w_down):
    n_rows, D = xs.shape
    Fd = w_gate.shape[2]
    tm = MOE_ROW_TILE
    G = n_rows // tm
    grid_spec = pltpu.PrefetchScalarGridSpec(
        num_scalar_prefetch=3,
        grid=(G,),
        in_specs=[
            pl.BlockSpec((tm, D), lambda i, te, xb, act: (xb[i], 0)),
            pl.BlockSpec((1, D, Fd), lambda i, te, xb, act: (te[i], 0, 0), pipeline_mode=pl.Buffered(1)),
            pl.BlockSpec((1, D, Fd), lambda i, te, xb, act: (te[i], 0, 0), pipeline_mode=pl.Buffered(1)),
            pl.BlockSpec((1, Fd, D), lambda i, te, xb, act: (te[i], 0, 0), pipeline_mode=pl.Buffered(1)),
        ],
        out_specs=pl.BlockSpec((tm, D), lambda i, te, xb, act: (i, 0)),
    )
    return pl.pallas_call(
        _expert_kernel,
        grid_spec=grid_spec,
        out_shape=jax.ShapeDtypeStruct((n_rows, D), F32),
        compiler_params=_params(("arbitrary",)),
        name="moe_experts",
    )(tile_expert, x_block, active, xs, w_gate, w_up, w_down)


def _finish_kernel(offs_ref, route_ref, h_ref, g_ref, ys_hbm, out_ref, ybuf, sem):
    j = pl.program_id(0)
    n = pl.num_programs(0)
    tm = h_ref.shape[0]
    slot = j % 2

    def copies(step_slot, step):
        return [_block_copy(ys_hbm.at[pl.ds(pl.multiple_of(offs_ref[step * N_EXPERTS + e], SUBLANES), tm)], ybuf.at[step_slot, e],
                            sem.at[step_slot])
                for e in range(N_EXPERTS)]

    @pl.when(j == 0)
    def _():
        for cp in copies(0, 0):
            cp.start()

    @pl.when(j + 1 < n)
    def _():
        for cp in copies(1 - slot, j + 1):
            cp.start()

    route = route_ref[...]
    one_hots = _slot_one_hot(route)
    for cp in copies(slot, j):
        cp.wait()
    acc = h_ref[...]
    for e in range(N_EXPERTS):
        rows = _dot(one_hots[e], ybuf[slot, e].astype(BF16))
        acc = acc + route[:, e:e + 1] * rows
    out_ref[...] = _rms(acc, g_ref[...])


def _moe_finish(ys, offs, route, h, g):
    T, D = h.shape
    tm = TOKEN_TILE
    grid_spec = pltpu.PrefetchScalarGridSpec(
        num_scalar_prefetch=1,
        grid=(T // tm,),
        in_specs=[
            pl.BlockSpec((tm, LANES), lambda j, o: (j, 0)),
            pl.BlockSpec((tm, D), lambda j, o: (j, 0)),
            pl.BlockSpec((1, D), lambda j, o: (0, 0)),
            pl.BlockSpec(memory_space=pl.ANY),
        ],
        out_specs=pl.BlockSpec((tm, D), lambda j, o: (j, 0)),
        scratch_shapes=[pltpu.VMEM((2, N_EXPERTS, tm, D), F32), pltpu.SemaphoreType.DMA((2,))],
    )
    return pl.pallas_call(
        _finish_kernel,
        grid_spec=grid_spec,
        out_shape=jax.ShapeDtypeStruct((T, D), F32),
        compiler_params=_params(("arbitrary",)),
        name="moe_finish",
    )(offs, route, h, g, ys)


def _route_metadata(route, tm):
    T = route.shape[0]
    nt = T // tm
    sel = (route[:, N_EXPERTS:2 * N_EXPERTS] > 0.5).astype(jnp.int32)
    cnt_tile = sel.reshape(nt, tm, N_EXPERTS).sum(axis=1)
    cnt_tile = ((cnt_tile + SUBLANES - 1) // SUBLANES) * SUBLANES
    n_e = cnt_tile.sum(axis=0)
    padded = ((n_e + tm - 1) // tm + 1) * tm
    ends = jnp.cumsum(padded)
    starts = ends - padded
    offs = starts[None, :] + jnp.cumsum(cnt_tile, axis=0) - cnt_tile
    fill = starts + n_e
    max_rows = TOP_K * T + nt * N_EXPERTS * (SUBLANES - 1)
    G = -(-max_rows // tm) + 2 * N_EXPERTS + 1
    tile_start = jnp.arange(G, dtype=jnp.int32) * tm
    tile_expert = jnp.minimum(jnp.sum((tile_start[:, None] >= ends[None, :]).astype(jnp.int32), axis=1),
                              N_EXPERTS - 1)
    active = (tile_start < fill[tile_expert]).astype(jnp.int32)
    x_block = jnp.where(active > 0, jnp.arange(G, dtype=jnp.int32), jnp.argmax(active).astype(jnp.int32))
    fill = jnp.concatenate([fill, ends[-1:] // tm])
    return (offs.reshape(-1).astype(jnp.int32), fill.astype(jnp.int32), tile_expert.astype(jnp.int32),
            x_block, active, G * tm)


def kernel(x, attn_norm, ffn_norm, hgrn_w_in, hgrn_lower_bounds, hgrn_out_norm, hgrn_w_out, kv_norm, w_kv,
           dil_w_q, dil_w_out, ffn_w_gate, ffn_w_up, ffn_w_down, moe_w_router, moe_w_gate, moe_w_up,
           moe_w_down, final_norm):
    B, S, D = x.shape
    row = lambda v: v.reshape(1, -1).astype(F32)

    lbs = jnp.cumsum(jax.nn.softmax(hgrn_lower_bounds.astype(F32), axis=0), axis=0)
    h = _hgrn_mixer(x, row(attn_norm[0]), hgrn_w_in[0].astype(BF16), row(lbs[0]), row(hgrn_out_norm[0]),
                    hgrn_w_out[0].astype(BF16))
    h = _dense_ffn(h, row(ffn_norm[0]), ffn_w_gate[0].astype(BF16), ffn_w_up[0].astype(BF16),
                   ffn_w_down[0].astype(BF16))

    perm = _pair_block_perm()
    perm3 = jnp.concatenate([perm + i * D for i in range(N_BRANCH)])
    cos_k, sin_k = _rope_tables(S, 1.0)
    cos_q, sin_q = _rope_tables(S, ATT_SCALE)
    w_k = w_kv[:, :N_BRANCH * D][:, perm3].astype(BF16)
    w_v = w_kv[:, N_BRANCH * D:].astype(BF16)
    w_q = dil_w_q[0][:, perm3].astype(BF16)
    ks = _norm_proj(h, row(kv_norm), w_k, cos_k, sin_k, rope=True)
    vs = _norm_proj(h, row(kv_norm), w_v, cos_k, sin_k, rope=False)
    qs = _norm_proj(h, row(attn_norm[1]), w_q, cos_q, sin_q, rope=True)
    os, lses = [], []
    for i, dil in enumerate(DILATIONS):
        o_i, lse_i = _dilated_attn(qs[i], ks[i], vs[i], dil)
        os.append(o_i)
        lses.append(lse_i)

    head_of_col = jnp.arange(D, dtype=jnp.int32) // ATT_HEAD_DIM
    expand = (jnp.arange(LANES, dtype=jnp.int32)[:, None] == head_of_col[None, :]).astype(BF16)
    wr = jnp.zeros((D, LANES), F32).at[:, :N_EXPERTS].set(moe_w_router[0].astype(F32))
    wr_hi = wr.astype(BF16)
    wr_lo = (wr - wr_hi.astype(F32)).astype(BF16)
    h, hn, route = _attn_combine(os, lses, h, dil_w_out[0].astype(BF16), expand, row(ffn_norm[1]), wr_hi, wr_lo)

    offs, fill, tile_expert, x_block, active, n_rows = _route_metadata(route, TOKEN_TILE)
    xs = _moe_dispatch(route, hn, offs, fill, n_rows)
    ys = _moe_experts(xs, tile_expert, x_block, active, moe_w_gate[0].astype(BF16), moe_w_up[0].astype(BF16),
                      moe_w_down[0].astype(BF16))
    out = _moe_finish(ys, offs, route, h, row(final_norm))
    return out.reshape(B, S, D)
```

```python
import functools

import jax
import jax.numpy as jnp
from jax import lax
from jax.experimental import pallas as pl
from jax.experimental.pallas import tpu as pltpu

F32 = jnp.float32
BF16 = jnp.bfloat16

D_MODEL = 1024
HGRN_HEAD_DIM = 128
HGRN_HEADS = D_MODEL // HGRN_HEAD_DIM
HGRN_CHUNK = 64
ATT_HEAD_DIM = 64
ATT_HEADS = D_MODEL // ATT_HEAD_DIM
DILATIONS = (1, 4, 16)
N_BRANCH = len(DILATIONS)
BAND = 128
ATT_SCALE = ATT_HEAD_DIM ** -0.5
ROPE_THETA = 10000.0
N_EXPERTS = 8
TOP_K = 2
EPS = 1e-6

LANES = 128
SUBLANES = 8
TOKEN_TILE = 256
MOE_ROW_TILE = TOKEN_TILE
VMEM_LIMIT = 56 * 1024 * 1024
NEG_BIG = -1e30
LN2 = 0.6931471805599453
LOG2E = 1.0 / LN2


def _dot(a, b):
    return jnp.dot(a, b, preferred_element_type=F32)


def _dot_nt(a, b):
    return lax.dot_general(a, b, (((1,), (1,)), ((), ())), preferred_element_type=F32)


def _dot_tn(a, b):
    return lax.dot_general(a, b, (((0,), (0,)), ((), ())), preferred_element_type=F32)


def _rms(x, g):
    return x * lax.rsqrt(jnp.mean(x * x, axis=-1, keepdims=True) + EPS) * g


def _sigmoid(x):
    return 1.0 / (1.0 + jnp.exp(-x))


def _silu(x):
    return x * _sigmoid(x)


def _split3(x):
    hi = x.astype(BF16)
    r1 = x - hi.astype(F32)
    mid = r1.astype(BF16)
    lo = (r1 - mid.astype(F32)).astype(BF16)
    return hi, mid, lo


def _const_spec(shape):
    nd = len(shape)
    return pl.BlockSpec(shape, lambda *_: (0,) * nd, pipeline_mode=pl.Buffered(1))


def _params(sem):
    return pltpu.CompilerParams(dimension_semantics=sem, vmem_limit_bytes=VMEM_LIMIT)


def _hgrn_kernel(x_ref, g_ref, win_ref, lb_ref, onorm_ref, wout_ref, out_ref,
                 state_ref, y_ref, o_ref, k_ref, b_ref, oi_ref, s_ref):
    tm = x_ref.shape[1]
    dk = HGRN_HEAD_DIM
    c_len = HGRN_CHUNK

    @pl.when(pl.program_id(1) == 0)
    def _():
        state_ref[...] = jnp.zeros_like(state_ref)

    x = x_ref[0]
    xn = _rms(x, g_ref[...]).astype(BF16)
    y_ref[...] = _dot(xn, win_ref[...])

    r = lax.broadcasted_iota(jnp.int32, (tm, tm), 0)
    c = lax.broadcasted_iota(jnp.int32, (tm, tm), 1)
    tri = jnp.where(c <= r, 1.0, 0.0).astype(BF16)

    lb = lb_ref[...]
    f = lb + (1.0 - lb) * _sigmoid(y_ref[:, D_MODEL:2 * D_MODEL])
    k_ref[...] = 1.0 - f
    lf_hi, lf_mid, lf_lo = _split3(jnp.log(f))
    b_ref[...] = _dot(tri, lf_hi) + _dot(tri, lf_mid) + _dot(tri, lf_lo)

    for h in range(HGRN_HEADS):
        cols = slice(h * dk, (h + 1) * dk)
        q = y_ref[:, h * dk:(h + 1) * dk]
        v16 = y_ref[:, 2 * D_MODEL + h * dk:2 * D_MODEL + (h + 1) * dk].astype(BF16)
        k = k_ref[:, cols]
        b = b_ref[:, cols]
        b_last = b[tm - 1:tm, :]
        st = state_ref[h]
        oi_ref[:, cols] = _dot_nt((q * jnp.exp(b)).astype(BF16), st.astype(BF16))
        k_end = (k * jnp.exp(b_last - b)).astype(BF16)
        state_ref[h] = st * jnp.exp(b_last) + _dot_tn(v16, k_end)
        for ci in range(tm // c_len):
            rows = slice(ci * c_len, (ci + 1) * c_len)
            n_keys = (ci + 1) * c_len
            ref = b[ci * c_len - 1:ci * c_len, :] if ci > 0 else jnp.zeros((1, dk), F32)
            q_dec = (q[rows] * jnp.exp(b[rows] - ref)).astype(BF16)
            k_dec = (k[:n_keys] * jnp.exp(ref - b[:n_keys])).astype(BF16)
            qpos = lax.broadcasted_iota(jnp.int32, (c_len, n_keys), 0) + ci * c_len
            kpos = lax.broadcasted_iota(jnp.int32, (c_len, n_keys), 1)
            s_ref[h, rows, :n_keys] = jnp.where(kpos <= qpos, _dot_nt(q_dec, k_dec), 0.0).astype(BF16)
            if n_keys < tm:
                s_ref[h, rows, n_keys:] = jnp.zeros((c_len, tm - n_keys), BF16)

    for h in range(HGRN_HEADS):
        cols = slice(h * dk, (h + 1) * dk)
        v16 = y_ref[:, 2 * D_MODEL + h * dk:2 * D_MODEL + (h + 1) * dk].astype(BF16)
        gate = y_ref[:, 3 * D_MODEL + h * dk:3 * D_MODEL + (h + 1) * dk]
        o = _dot(s_ref[h], v16) + oi_ref[:, cols]
        o_ref[:, cols] = (_rms(o, onorm_ref[...]) * _silu(gate)).astype(BF16)

    out_ref[0] = x + _dot(o_ref[...], wout_ref[...])


def _hgrn_mixer(x, g, w_in, lb, out_norm, w_out):
    B, S, D = x.shape
    tm = TOKEN_TILE
    return pl.pallas_call(
        _hgrn_kernel,
        grid=(B, S // tm),
        in_specs=[
            pl.BlockSpec((1, tm, D), lambda b, s: (b, s, 0)),
            _const_spec((1, D)),
            _const_spec((D, 4 * D)),
            _const_spec((1, D)),
            _const_spec((1, HGRN_HEAD_DIM)),
            _const_spec((D, D)),
        ],
        out_specs=pl.BlockSpec((1, tm, D), lambda b, s: (b, s, 0)),
        out_shape=jax.ShapeDtypeStruct((B, S, D), F32),
        scratch_shapes=[
            pltpu.VMEM((HGRN_HEADS, HGRN_HEAD_DIM, HGRN_HEAD_DIM), F32),
            pltpu.VMEM((tm, 4 * D), F32),
            pltpu.VMEM((tm, D), BF16),
            pltpu.VMEM((tm, D), F32),
            pltpu.VMEM((tm, D), F32),
            pltpu.VMEM((tm, D), F32),
            pltpu.VMEM((HGRN_HEADS, tm, tm), BF16),
        ],
        compiler_params=_params(("arbitrary", "arbitrary")),
        name="hgrn_mixer",
    )(x, g, w_in, lb, out_norm, w_out)


def _ffn_kernel(x_ref, g_ref, wg_ref, wu_ref, wd_ref, out_ref):
    x = x_ref[0]
    xn = _rms(x, g_ref[...]).astype(BF16)
    a = _silu(_dot(xn, wg_ref[...])) * _dot(xn, wu_ref[...])
    out_ref[0] = x + _dot(a.astype(BF16), wd_ref[...])


def _dense_ffn(x, g, w_gate, w_up, w_down):
    B, S, D = x.shape
    Fd = w_gate.shape[1]
    tm = TOKEN_TILE
    return pl.pallas_call(
        _ffn_kernel,
        grid=(B, S // tm),
        in_specs=[
            pl.BlockSpec((1, tm, D), lambda b, s: (b, s, 0)),
            _const_spec((1, D)),
            _const_spec((D, Fd)),
            _const_spec((D, Fd)),
            _const_spec((Fd, D)),
        ],
        out_specs=pl.BlockSpec((1, tm, D), lambda b, s: (b, s, 0)),
        out_shape=jax.ShapeDtypeStruct((B, S, D), F32),
        compiler_params=_params(("arbitrary", "arbitrary")),
        name="dense_ffn",
    )(x, g, w_gate, w_up, w_down)


def _proj_kernel(x_ref, g_ref, w_ref, cos_ref, sin_ref, o0_ref, o1_ref, o2_ref, t_ref, *, rope):
    tm = x_ref.shape[1]
    xn = _rms(x_ref[0], g_ref[...]).astype(BF16)
    out_refs = (o0_ref, o1_ref, o2_ref)
    for j, dil in enumerate(DILATIONS):
        t = _dot(xn, w_ref[:, j * D_MODEL:(j + 1) * D_MODEL])
        for p in range(D_MODEL // LANES):
            cols = slice(p * LANES, (p + 1) * LANES)
            tp = t[:, cols]
            if rope:
                tp = tp * cos_ref[...] + pltpu.roll(tp, LANES // 2, 1) * sin_ref[...]
            if dil == 1:
                out_refs[j][0, 0, :, cols] = tp.astype(BF16)
            else:
                t_ref[p] = tp
                for r in range(dil):
                    out_refs[j][0, r, :, cols] = t_ref[p, pl.ds(r, tm // dil, stride=dil), :].astype(BF16)


def _norm_proj(x, g, w, cos, sin, rope):
    B, S, D = x.shape
    tm = TOKEN_TILE
    return pl.pallas_call(
        functools.partial(_proj_kernel, rope=rope),
        grid=(B, S // tm),
        in_specs=[
            pl.BlockSpec((1, tm, D), lambda b, s: (b, s, 0)),
            _const_spec((1, D)),
            _const_spec((D, N_BRANCH * D)),
            pl.BlockSpec((tm, LANES), lambda b, s: (s, 0)),
            pl.BlockSpec((tm, LANES), lambda b, s: (s, 0)),
        ],
        out_specs=[pl.BlockSpec((1, dil, tm // dil, D), lambda b, s: (b, 0, s, 0)) for dil in DILATIONS],
        out_shape=[jax.ShapeDtypeStruct((B, dil, S // dil, D), BF16) for dil in DILATIONS],
        scratch_shapes=[pltpu.VMEM((D // LANES, tm, LANES), F32)],
        compiler_params=_params(("arbitrary", "arbitrary")),
        name="norm_proj_rope" if rope else "norm_proj",
    )(x, g, w, cos, sin)


def _rope_tables(seq_len, scale):
    half = ATT_HEAD_DIM // 2
    inv_freq = ROPE_THETA ** (-jnp.arange(half, dtype=F32) / half)
    ang = jnp.arange(seq_len, dtype=F32)[:, None] * inv_freq[None, :]
    cos = jnp.tile(jnp.cos(ang), (1, LANES // half))
    sin = jnp.tile(jnp.sin(ang), (1, LANES // half))
    sign = jnp.where(jnp.arange(LANES) < LANES // 2, -1.0, 1.0).astype(F32)
    return cos * scale, sin * sign[None, :] * scale


def _pair_block_perm():
    half = ATT_HEAD_DIM // 2
    idx = []
    for p in range(ATT_HEADS // 2):
        base = p * LANES
        for part in range(2):
            for hh in range(2):
                start = base + hh * ATT_HEAD_DIM + part * half
                idx.extend(range(start, start + half))
    return jnp.asarray(idx, dtype=jnp.int32)


def _attn_kernel(q_ref, kp_ref, kc_ref, vp_ref, vc_ref, o_ref, lse_ref):
    n = pl.program_id(2)
    qi = lax.broadcasted_iota(jnp.int32, (BAND, 2 * BAND), 0)
    ki = lax.broadcasted_iota(jnp.int32, (BAND, 2 * BAND), 1)
    dist = qi + BAND - ki
    first_key = jnp.where(n > 0, 0, BAND)
    valid = (dist >= 0) & (dist <= BAND) & (ki >= first_key)
    lane = lax.broadcasted_iota(jnp.int32, (BAND, LANES), 1)
    lse_tile = jnp.zeros((BAND, LANES), F32)
    for p in range(ATT_HEADS // 2):
        cols = slice(p * LANES, (p + 1) * LANES)
        qp = q_ref[:, cols]
        kcat = jnp.concatenate([kp_ref[:, cols], kc_ref[:, cols]], axis=0)
        vcat = jnp.concatenate([vp_ref[:, cols], vc_ref[:, cols]], axis=0)
        o_pair = jnp.zeros((BAND, LANES), F32)
        for a in range(2):
            qm = jnp.where((lane // (ATT_HEAD_DIM // 2)) % 2 == a, qp, jnp.zeros_like(qp))
            s = jnp.where(valid, _dot_nt(qm, kcat), NEG_BIG)
            m = jnp.max(s, axis=-1, keepdims=True)
            pr = jnp.exp2(s - m)
            l = jnp.sum(pr, axis=-1, keepdims=True)
            o = _dot(pr.astype(BF16), vcat) / l
            o_pair = jnp.where(lane // ATT_HEAD_DIM == a, o, o_pair)
            lse_tile = jnp.where(lane == 2 * p + a, m * LN2 + jnp.log(l), lse_tile)
        o_ref[:, cols] = o_pair.astype(o_ref.dtype)
    lse_ref[...] = lse_tile


def _dilated_attn(q, k, v, dil):
    B, _, L, D = q.shape
    nb = L // BAND
    blk = (None, None, BAND, D)
    cur = lambda b, r, n: (b, r, n, 0)
    prev = lambda b, r, n: (b, r, jnp.maximum(n - 1, 0), 0)
    return pl.pallas_call(
        _attn_kernel,
        grid=(B, dil, nb),
        in_specs=[
            pl.BlockSpec(blk, cur),
            pl.BlockSpec(blk, prev),
            pl.BlockSpec(blk, cur),
            pl.BlockSpec(blk, prev),
            pl.BlockSpec(blk, cur),
        ],
        out_specs=[pl.BlockSpec(blk, cur), pl.BlockSpec((None, None, BAND, LANES), cur)],
        out_shape=[
            jax.ShapeDtypeStruct((B, dil, L, D), BF16),
            jax.ShapeDtypeStruct((B, dil, L, LANES), F32),
        ],
        compiler_params=_params(("arbitrary", "arbitrary", "arbitrary")),
        name=f"dilated_attn_d{dil}",
    )(q, k, k, v, v)


def _combine_kernel(o0_ref, o1_ref, o2_ref, l0_ref, l1_ref, l2_ref, h_ref, wout_ref, expand_ref,
                    g_ref, wr_ref, h_out_ref, hn_out_ref, route_ref, o_scr, l_scr):
    tm = h_ref.shape[1]
    o_refs = (o0_ref, o1_ref, o2_ref)
    l_refs = (l0_ref, l1_ref, l2_ref)
    for i, dil in enumerate(DILATIONS):
        for r in range(dil):
            rows = pl.ds(r, tm // dil, stride=dil)
            l_scr[i, rows, :] = l_refs[i][0, r]
            for p in range(D_MODEL // LANES):
                o_scr[i, p, rows, :] = o_refs[i][0, r, :, p * LANES:(p + 1) * LANES].astype(F32)
    lses = [l_scr[i] for i in range(N_BRANCH)]
    m = jnp.maximum(jnp.maximum(lses[0], lses[1]), lses[2])
    es = [jnp.exp(l - m) for l in lses]
    den = es[0] + es[1] + es[2]
    o = None
    for i in range(N_BRANCH):
        w = es[i] / den
        w_hi = w.astype(BF16)
        w_lo = (w - w_hi.astype(F32)).astype(BF16)
        w_full = _dot(jnp.concatenate([w_hi, w_lo], axis=1), expand_ref[...])
        term = w_full * jnp.concatenate([o_scr[i, p] for p in range(D_MODEL // LANES)], axis=1)
        o = term if o is None else o + term
    h = h_ref[0] + _dot(o.astype(BF16), wout_ref[...])
    h_out_ref[...] = h
    hn = _rms(h, g_ref[...])
    hn_hi = hn.astype(BF16)
    hn_out_ref[...] = hn_hi

    hn_lo = (hn - hn_hi.astype(F32)).astype(BF16)
    l_hi = _dot(hn_hi, wr_ref[...])
    logits = l_hi[:, :LANES] + l_hi[:, LANES:] + _dot(hn_lo, wr_ref[:, :LANES])
    lane = lax.broadcasted_iota(jnp.int32, logits.shape, 1)
    lane_f = lane.astype(F32)
    logits = jnp.where(lane < N_EXPERTS, logits, NEG_BIG)
    mx = jnp.max(logits, axis=-1, keepdims=True)
    pe = jnp.exp(logits - mx)
    probs = pe / jnp.sum(pe, axis=-1, keepdims=True)
    probs = jnp.where(lane < N_EXPERTS, probs, -1.0)
    p1 = jnp.max(probs, axis=-1, keepdims=True)
    i1 = jnp.min(jnp.where(probs == p1, lane_f, float(LANES)), axis=-1, keepdims=True)
    rest = jnp.where(lane_f == i1, -1.0, probs)
    p2 = jnp.max(rest, axis=-1, keepdims=True)
    i2 = jnp.min(jnp.where(rest == p2, lane_f, float(LANES)), axis=-1, keepdims=True)
    tot = p1 + p2
    route = jnp.where(lane_f == i1, p1 / tot, 0.0)
    route = jnp.where(lane_f == i2, p2 / tot, route)
    sel = jnp.where((lane_f == i1 + N_EXPERTS) | (lane_f == i2 + N_EXPERTS), 1.0, 0.0)
    route_ref[...] = route + sel


def _attn_combine(os, lses, h, w_out, expand, g, wr):
    B, S, D = h.shape
    T = B * S
    tm = TOKEN_TILE
    ts = S // tm
    row = lambda b, s: (b * ts + s, 0)
    o_specs = [pl.BlockSpec((1, dil, tm // dil, D), lambda b, s: (b, 0, s, 0)) for dil in DILATIONS]
    l_specs = [pl.BlockSpec((1, dil, tm // dil, LANES), lambda b, s: (b, 0, s, 0)) for dil in DILATIONS]
    return pl.pallas_call(
        _combine_kernel,
        grid=(B, ts),
        in_specs=o_specs + l_specs + [
            pl.BlockSpec((1, tm, D), lambda b, s: (b, s, 0)),
            _const_spec((D, D)),
            _const_spec((2 * LANES, D)),
            _const_spec((1, D)),
            _const_spec((D, 2 * LANES)),
        ],
        out_specs=[pl.BlockSpec((tm, D), row), pl.BlockSpec((tm, D), row), pl.BlockSpec((tm, LANES), row)],
        out_shape=[
            jax.ShapeDtypeStruct((T, D), F32),
            jax.ShapeDtypeStruct((T, D), BF16),
            jax.ShapeDtypeStruct((T, LANES), F32),
        ],
        scratch_shapes=[pltpu.VMEM((N_BRANCH, D // LANES, tm, LANES), F32),
                        pltpu.VMEM((N_BRANCH, tm, LANES), F32)],
        compiler_params=_params(("arbitrary", "arbitrary")),
        name="attn_combine_route",
    )(*os, *lses, h, w_out, expand, g, wr)


def _slot_one_hot(route):
    tm = route.shape[0]
    lane = lax.broadcasted_iota(jnp.int32, route.shape, 1)
    sel = jnp.where((lane >= N_EXPERTS) & (lane < 2 * N_EXPERTS), route, 0.0)
    r = lax.broadcasted_iota(jnp.int32, (tm, tm), 0)
    c = lax.broadcasted_iota(jnp.int32, (tm, tm), 1)
    strict_lower = jnp.where(c < r, 1.0, 0.0).astype(BF16)
    rank = _dot(strict_lower, sel.astype(BF16))
    slot_iota = lax.broadcasted_iota(jnp.int32, (tm, tm), 1).astype(F32)
    out = []
    for e in range(N_EXPERTS):
        col = N_EXPERTS + e
        hit = (slot_iota == rank[:, col:col + 1]) & (sel[:, col:col + 1] > 0.5)
        out.append(jnp.where(hit, 1.0, 0.0).astype(BF16))
    return out


HALF_D = D_MODEL // 2
HI_MASK = 0xFFFF0000


def _pack_rows(x):
    lo = pltpu.bitcast(x[:, :HALF_D], jnp.uint32)
    hi = pltpu.bitcast(x[:, HALF_D:], jnp.uint32)
    return (hi & jnp.uint32(HI_MASK)) | (lo >> jnp.uint32(16))


def _unpack_rows(w):
    lo = pltpu.bitcast(w << jnp.uint32(16), F32)
    hi = pltpu.bitcast(w & jnp.uint32(HI_MASK), F32)
    return jnp.concatenate([lo, hi], axis=1).astype(BF16)


def _block_copy(src, dst, sem):
    return pltpu.make_async_copy(src, dst, sem)


def _dispatch_kernel(offs_ref, fill_ref, route_ref, hn_ref, xs_hbm, stage, sem):
    j = pl.program_id(0)
    n = pl.num_programs(0)
    tm = hn_ref.shape[0]
    n_tiles = xs_hbm.shape[0] // tm
    slot = j % 2

    def zero_copy(row):
        return _block_copy(stage.at[1, 0], xs_hbm.at[pl.ds(pl.multiple_of(row, SUBLANES), tm)], sem.at[1])

    @pl.when(j == 0)
    def _():
        stage[1, 0] = jnp.zeros((tm, HALF_D), jnp.uint32)
        first_tail = fill_ref[N_EXPERTS]

        def start_tail(t, carry):
            zero_copy(t * tm).start()
            return carry

        def wait_tail(t, carry):
            zero_copy(t * tm).wait()
            return carry

        first = [zero_copy(fill_ref[e]) for e in range(N_EXPERTS)]
        second = [zero_copy(fill_ref[e] + tm) for e in range(N_EXPERTS)]
        for cp in first:
            cp.start()
        lax.fori_loop(first_tail, n_tiles, start_tail, 0)
        for cp in first:
            cp.wait()
        lax.fori_loop(first_tail, n_tiles, wait_tail, 0)
        for cp in second:
            cp.start()
        for cp in second:
            cp.wait()

    hn = hn_ref[...]
    one_hots = _slot_one_hot(route_ref[...])
    for e in range(N_EXPERTS):
        stage[slot, e] = _pack_rows(_dot_tn(one_hots[e], hn))

    def copies(step_slot, step):
        return [_block_copy(stage.at[step_slot, e],
                            xs_hbm.at[pl.ds(pl.multiple_of(offs_ref[step * N_EXPERTS + e], SUBLANES), tm)],
                            sem.at[step_slot])
                for e in range(N_EXPERTS)]

    @pl.when(j > 0)
    def _():
        for cp in copies(1 - slot, j - 1):
            cp.wait()

    for cp in copies(slot, j):
        cp.start()

    @pl.when(j == n - 1)
    def _():
        for cp in copies(slot, j):
            cp.wait()


def _moe_dispatch(route, hn, offs, fill, n_rows):
    T, D = hn.shape
    tm = TOKEN_TILE
    grid_spec = pltpu.PrefetchScalarGridSpec(
        num_scalar_prefetch=2,
        grid=(T // tm,),
        in_specs=[
            pl.BlockSpec((tm, LANES), lambda j, o, f: (j, 0)),
            pl.BlockSpec((tm, D), lambda j, o, f: (j, 0)),
        ],
        out_specs=pl.BlockSpec(memory_space=pl.ANY),
        scratch_shapes=[pltpu.VMEM((2, N_EXPERTS, tm, HALF_D), jnp.uint32), pltpu.SemaphoreType.DMA((2,))],
    )
    return pl.pallas_call(
        _dispatch_kernel,
        grid_spec=grid_spec,
        out_shape=jax.ShapeDtypeStruct((n_rows, HALF_D), jnp.uint32),
        compiler_params=_params(("arbitrary",)),
        name="moe_dispatch",
    )(offs, fill, route, hn)


def _expert_kernel(te_ref, xb_ref, act_ref, x_ref, wg_ref, wu_ref, wd_ref, y_ref):
    del te_ref, xb_ref
    i = pl.program_id(0)
    tm = x_ref.shape[0]

    @pl.when(act_ref[i] > 0)
    def _():
        x = _unpack_rows(x_ref[...])
        f_dim = wg_ref.shape[2]
        fc = 512
        acc = jnp.zeros((tm, D_MODEL), F32)
        for c in range(f_dim // fc):
            cs = slice(c * fc, (c + 1) * fc)
            a = _silu(_dot(x, wg_ref[0, :, cs])) * _dot(x, wu_ref[0, :, cs])
            acc = acc + _dot(a.astype(BF16), wd_ref[0, cs, :])
        y_ref[...] = _pack_rows(acc.astype(BF16).astype(F32))

    @pl.when(act_ref[i] == 0)
    def _():
        y_ref[...] = jnp.zeros_like(y_ref)


def _moe_experts(xs, tile_expert, x_block, active, w_gate, w_up, w_down):
    n_rows = xs.shape[0]
    D = D_MODEL
    Fd = w_gate.shape[2]
    tm = MOE_ROW_TILE
    G = n_rows // tm
    grid_spec = pltpu.PrefetchScalarGridSpec(
        num_scalar_prefetch=3,
        grid=(G,),
        in_specs=[
            pl.BlockSpec((tm, HALF_D), lambda i, te, xb, act: (xb[i], 0)),
            pl.BlockSpec((1, D, Fd), lambda i, te, xb, act: (te[i], 0, 0), pipeline_mode=pl.Buffered(1)),
            pl.BlockSpec((1, D, Fd), lambda i, te, xb, act: (te[i], 0, 0), pipeline_mode=pl.Buffered(1)),
            pl.BlockSpec((1, Fd, D), lambda i, te, xb, act: (te[i], 0, 0), pipeline_mode=pl.Buffered(1)),
        ],
        out_specs=pl.BlockSpec((tm, HALF_D), lambda i, te, xb, act: (i, 0)),
    )
    return pl.pallas_call(
        _expert_kernel,
        grid_spec=grid_spec,
        out_shape=jax.ShapeDtypeStruct((n_rows, HALF_D), jnp.uint32),
        compiler_params=_params(("arbitrary",)),
        name="moe_experts",
    )(tile_expert, x_block, active, xs, w_gate, w_up, w_down)


def _finish_kernel(offs_ref, route_ref, h_ref, g_ref, ys_hbm, out_ref, ybuf, sem):
    j = pl.program_id(0)
    n = pl.num_programs(0)
    tm = h_ref.shape[0]
    slot = j % 2

    def copies(step_slot, step):
        return [_block_copy(ys_hbm.at[pl.ds(pl.multiple_of(offs_ref[step * N_EXPERTS + e], SUBLANES), tm)], ybuf.at[step_slot, e],
                            sem.at[step_slot])
                for e in range(N_EXPERTS)]

    @pl.when(j == 0)
    def _():
        for cp in copies(0, 0):
            cp.start()

    @pl.when(j + 1 < n)
    def _():
        for cp in copies(1 - slot, j + 1):
            cp.start()

    route = route_ref[...]
    one_hots = _slot_one_hot(route)
    for cp in copies(slot, j):
        cp.wait()
    acc = h_ref[...]
    for e in range(N_EXPERTS):
        rows = _dot(one_hots[e], _unpack_rows(ybuf[slot, e]))
        acc = acc + route[:, e:e + 1] * rows
    out_ref[...] = _rms(acc, g_ref[...])


def _moe_finish(ys, offs, route, h, g):
    T, D = h.shape
    tm = TOKEN_TILE
    grid_spec = pltpu.PrefetchScalarGridSpec(
        num_scalar_prefetch=1,
        grid=(T // tm,),
        in_specs=[
            pl.BlockSpec((tm, LANES), lambda j, o: (j, 0)),
            pl.BlockSpec((tm, D), lambda j, o: (j, 0)),
            pl.BlockSpec((1, D), lambda j, o: (0, 0)),
            pl.BlockSpec(memory_space=pl.ANY),
        ],
        out_specs=pl.BlockSpec((tm, D), lambda j, o: (j, 0)),
        scratch_shapes=[pltpu.VMEM((2, N_EXPERTS, tm, HALF_D), jnp.uint32), pltpu.SemaphoreType.DMA((2,))],
    )
    return pl.pallas_call(
        _finish_kernel,
        grid_spec=grid_spec,
        out_shape=jax.ShapeDtypeStruct((T, D), F32),
        compiler_params=_params(("arbitrary",)),
        name="moe_finish",
    )(offs, route, h, g, ys)


def _route_metadata(route, tm):
    T = route.shape[0]
    nt = T // tm
    sel = (route[:, N_EXPERTS:2 * N_EXPERTS] > 0.5).astype(jnp.int32)
    cnt_tile = sel.reshape(nt, tm, N_EXPERTS).sum(axis=1)
    cnt_tile = ((cnt_tile + SUBLANES - 1) // SUBLANES) * SUBLANES
    n_e = cnt_tile.sum(axis=0)
    padded = ((n_e + tm - 1) // tm + 1) * tm
    ends = jnp.cumsum(padded)
    starts = ends - padded
    offs = starts[None, :] + jnp.cumsum(cnt_tile, axis=0) - cnt_tile
    fill = starts + n_e
    max_rows = TOP_K * T + nt * N_EXPERTS * (SUBLANES - 1)
    G = -(-max_rows // tm) + 2 * N_EXPERTS + 1
    tile_start = jnp.arange(G, dtype=jnp.int32) * tm
    tile_expert = jnp.minimum(jnp.sum((tile_start[:, None] >= ends[None, :]).astype(jnp.int32), axis=1),
                              N_EXPERTS - 1)
    active = (tile_start < fill[tile_expert]).astype(jnp.int32)
    x_block = jnp.where(active > 0, jnp.arange(G, dtype=jnp.int32), jnp.argmax(active).astype(jnp.int32))
    fill = jnp.concatenate([fill, ends[-1:] // tm])
    return (offs.reshape(-1).astype(jnp.int32), fill.astype(jnp.int32), tile_expert.astype(jnp.int32),
            x_block, active, G * tm)


def kernel(x, attn_norm, ffn_norm, hgrn_w_in, hgrn_lower_bounds, hgrn_out_norm, hgrn_w_out, kv_norm, w_kv,
           dil_w_q, dil_w_out, ffn_w_gate, ffn_w_up, ffn_w_down, moe_w_router, moe_w_gate, moe_w_up,
           moe_w_down, final_norm):
    B, S, D = x.shape
    row = lambda v: v.reshape(1, -1).astype(F32)

    lbs = jnp.cumsum(jax.nn.softmax(hgrn_lower_bounds.astype(F32), axis=0), axis=0)
    h = _hgrn_mixer(x, row(attn_norm[0]), hgrn_w_in[0].astype(BF16), row(lbs[0]), row(hgrn_out_norm[0]),
                    hgrn_w_out[0].astype(BF16))
    h = _dense_ffn(h, row(ffn_norm[0]), ffn_w_gate[0].astype(BF16), ffn_w_up[0].astype(BF16),
                   ffn_w_down[0].astype(BF16))

    perm = _pair_block_perm()
    perm3 = jnp.concatenate([perm + i * D for i in range(N_BRANCH)])
    cos_k, sin_k = _rope_tables(S, 1.0)
    cos_q, sin_q = _rope_tables(S, ATT_SCALE * LOG2E)
    w_k = w_kv[:, :N_BRANCH * D][:, perm3].astype(BF16)
    w_v = w_kv[:, N_BRANCH * D:].astype(BF16)
    w_q = dil_w_q[0][:, perm3].astype(BF16)
    ks = _norm_proj(h, row(kv_norm), w_k, cos_k, sin_k, rope=True)
    vs = _norm_proj(h, row(kv_norm), w_v, cos_k, sin_k, rope=False)
    qs = _norm_proj(h, row(attn_norm[1]), w_q, cos_q, sin_q, rope=True)
    os, lses = [], []
    for i, dil in enumerate(DILATIONS):
        o_i, lse_i = _dilated_attn(qs[i], ks[i], vs[i], dil)
        os.append(o_i)
        lses.append(lse_i)

    head_of_col = jnp.arange(D, dtype=jnp.int32) // ATT_HEAD_DIM
    expand = (jnp.arange(LANES, dtype=jnp.int32)[:, None] == head_of_col[None, :]).astype(BF16)
    expand = jnp.concatenate([expand, expand], axis=0)
    wr = jnp.zeros((D, LANES), F32).at[:, :N_EXPERTS].set(moe_w_router[0].astype(F32))
    wr_hi = wr.astype(BF16)
    wr_lo = (wr - wr_hi.astype(F32)).astype(BF16)
    h, hn, route = _attn_combine(os, lses, h, dil_w_out[0].astype(BF16), expand, row(ffn_norm[1]),
                                 jnp.concatenate([wr_hi, wr_lo], axis=1))

    offs, fill, tile_expert, x_block, active, n_rows = _route_metadata(route, TOKEN_TILE)
    xs = _moe_dispatch(route, hn, offs, fill, n_rows)
    ys = _moe_experts(xs, tile_expert, x_block, active, moe_w_gate[0].astype(BF16), moe_w_up[0].astype(BF16),
                      moe_w_down[0].astype(BF16))
    out = _moe_finish(ys, offs, route, h, row(final_norm))
    return out.reshape(B, S, D)
```

```python
import functools

import jax
import jax.numpy as jnp
from jax import lax
from jax.experimental import pallas as pl
from jax.experimental.pallas import tpu as pltpu

F32 = jnp.float32
BF16 = jnp.bfloat16

D_MODEL = 1024
HGRN_HEAD_DIM = 128
HGRN_HEADS = D_MODEL // HGRN_HEAD_DIM
HGRN_CHUNK = 64
ATT_HEAD_DIM = 64
ATT_HEADS = D_MODEL // ATT_HEAD_DIM
DILATIONS = (1, 4, 16)
N_BRANCH = len(DILATIONS)
BAND = 128
ATT_SCALE = ATT_HEAD_DIM ** -0.5
ROPE_THETA = 10000.0
N_EXPERTS = 8
TOP_K = 2
EPS = 1e-6

LANES = 128
SUBLANES = 8
TOKEN_TILE = 256
MOE_ROW_TILE = TOKEN_TILE
VMEM_LIMIT = 56 * 1024 * 1024
NEG_BIG = -1e30
LN2 = 0.6931471805599453
LOG2E = 1.0 / LN2


def _dot(a, b):
    return jnp.dot(a, b, preferred_element_type=F32)


def _dot_nt(a, b):
    return lax.dot_general(a, b, (((1,), (1,)), ((), ())), preferred_element_type=F32)


def _dot_tn(a, b):
    return lax.dot_general(a, b, (((0,), (0,)), ((), ())), preferred_element_type=F32)


def _rms(x, g):
    return x * lax.rsqrt(jnp.mean(x * x, axis=-1, keepdims=True) + EPS) * g


def _sigmoid(x):
    return 1.0 / (1.0 + jnp.exp(-x))


def _silu(x):
    return x * _sigmoid(x)


def _split3(x):
    hi = x.astype(BF16)
    r1 = x - hi.astype(F32)
    mid = r1.astype(BF16)
    lo = (r1 - mid.astype(F32)).astype(BF16)
    return hi, mid, lo


def _const_spec(shape):
    nd = len(shape)
    return pl.BlockSpec(shape, lambda *_: (0,) * nd, pipeline_mode=pl.Buffered(1))


def _params(sem):
    return pltpu.CompilerParams(dimension_semantics=sem, vmem_limit_bytes=VMEM_LIMIT)


def _hgrn_kernel(x_ref, g_ref, win_ref, lb_ref, onorm_ref, wout_ref, out_ref,
                 state_ref, y_ref, o_ref, k_ref, b_ref, oi_ref, s_ref):
    tm = x_ref.shape[1]
    dk = HGRN_HEAD_DIM
    c_len = HGRN_CHUNK

    @pl.when(pl.program_id(1) == 0)
    def _():
        state_ref[...] = jnp.zeros_like(state_ref)

    x = x_ref[0]
    xn = _rms(x, g_ref[...]).astype(BF16)
    y_ref[...] = _dot(xn, win_ref[...])

    r = lax.broadcasted_iota(jnp.int32, (tm, tm), 0)
    c = lax.broadcasted_iota(jnp.int32, (tm, tm), 1)
    tri = jnp.where(c <= r, 1.0, 0.0).astype(BF16)

    lb = lb_ref[...]
    f = lb + (1.0 - lb) * _sigmoid(y_ref[:, D_MODEL:2 * D_MODEL])
    k_ref[...] = 1.0 - f
    lf_hi, lf_mid, lf_lo = _split3(jnp.log(f))
    b_ref[...] = _dot(tri, lf_hi) + _dot(tri, lf_mid) + _dot(tri, lf_lo)

    for h in range(HGRN_HEADS):
        cols = slice(h * dk, (h + 1) * dk)
        q = y_ref[:, h * dk:(h + 1) * dk]
        v16 = y_ref[:, 2 * D_MODEL + h * dk:2 * D_MODEL + (h + 1) * dk].astype(BF16)
        k = k_ref[:, cols]
        b = b_ref[:, cols]
        b_last = b[tm - 1:tm, :]
        st = state_ref[h]
        oi_ref[:, cols] = _dot_nt((q * jnp.exp(b)).astype(BF16), st.astype(BF16))
        k_end = (k * jnp.exp(b_last - b)).astype(BF16)
        state_ref[h] = st * jnp.exp(b_last) + _dot_tn(v16, k_end)
        for ci in range(tm // c_len):
            rows = slice(ci * c_len, (ci + 1) * c_len)
            n_keys = (ci + 1) * c_len
            ref = b[ci * c_len - 1:ci * c_len, :] if ci > 0 else jnp.zeros((1, dk), F32)
            q_dec = (q[rows] * jnp.exp(b[rows] - ref)).astype(BF16)
            k_dec = (k[:n_keys] * jnp.exp(ref - b[:n_keys])).astype(BF16)
            qpos = lax.broadcasted_iota(jnp.int32, (c_len, n_keys), 0) + ci * c_len
            kpos = lax.broadcasted_iota(jnp.int32, (c_len, n_keys), 1)
            s_ref[h, rows, :n_keys] = jnp.where(kpos <= qpos, _dot_nt(q_dec, k_dec), 0.0).astype(BF16)
            if n_keys < tm:
                s_ref[h, rows, n_keys:] = jnp.zeros((c_len, tm - n_keys), BF16)

    for h in range(HGRN_HEADS):
        cols = slice(h * dk, (h + 1) * dk)
        v16 = y_ref[:, 2 * D_MODEL + h * dk:2 * D_MODEL + (h + 1) * dk].astype(BF16)
        gate = y_ref[:, 3 * D_MODEL + h * dk:3 * D_MODEL + (h + 1) * dk]
        o = _dot(s_ref[h], v16) + oi_ref[:, cols]
        o_ref[:, cols] = (_rms(o, onorm_ref[...]) * _silu(gate)).astype(BF16)

    out_ref[0] = x + _dot(o_ref[...], wout_ref[...])


def _hgrn_mixer(x, g, w_in, lb, out_norm, w_out):
    B, S, D = x.shape
    tm = TOKEN_TILE
    return pl.pallas_call(
        _hgrn_kernel,
        grid=(B, S // tm),
        in_specs=[
            pl.BlockSpec((1, tm, D), lambda b, s: (b, s, 0)),
            _const_spec((1, D)),
            _const_spec((D, 4 * D)),
            _const_spec((1, D)),
            _const_spec((1, HGRN_HEAD_DIM)),
            _const_spec((D, D)),
        ],
        out_specs=pl.BlockSpec((1, tm, D), lambda b, s: (b, s, 0)),
        out_shape=jax.ShapeDtypeStruct((B, S, D), F32),
        scratch_shapes=[
            pltpu.VMEM((HGRN_HEADS, HGRN_HEAD_DIM, HGRN_HEAD_DIM), F32),
            pltpu.VMEM((tm, 4 * D), F32),
            pltpu.VMEM((tm, D), BF16),
            pltpu.VMEM((tm, D), F32),
            pltpu.VMEM((tm, D), F32),
            pltpu.VMEM((tm, D), F32),
            pltpu.VMEM((HGRN_HEADS, tm, tm), BF16),
        ],
        compiler_params=_params(("arbitrary", "arbitrary")),
        name="hgrn_mixer",
    )(x, g, w_in, lb, out_norm, w_out)


def _ffn_kernel(x_ref, g_ref, wg_ref, wu_ref, wd_ref, out_ref):
    x = x_ref[0]
    xn = _rms(x, g_ref[...]).astype(BF16)
    a = _silu(_dot(xn, wg_ref[...])) * _dot(xn, wu_ref[...])
    out_ref[0] = x + _dot(a.astype(BF16), wd_ref[...])


def _dense_ffn(x, g, w_gate, w_up, w_down):
    B, S, D = x.shape
    Fd = w_gate.shape[1]
    tm = TOKEN_TILE
    return pl.pallas_call(
        _ffn_kernel,
        grid=(B, S // tm),
        in_specs=[
            pl.BlockSpec((1, tm, D), lambda b, s: (b, s, 0)),
            _const_spec((1, D)),
            _const_spec((D, Fd)),
            _const_spec((D, Fd)),
            _const_spec((Fd, D)),
        ],
        out_specs=pl.BlockSpec((1, tm, D), lambda b, s: (b, s, 0)),
        out_shape=jax.ShapeDtypeStruct((B, S, D), F32),
        compiler_params=_params(("arbitrary", "arbitrary")),
        name="dense_ffn",
    )(x, g, w_gate, w_up, w_down)


def _proj_kernel(x_ref, g_ref, w_ref, cos_ref, sin_ref, o0_ref, o1_ref, o2_ref, t_ref, *, rope):
    tm = x_ref.shape[1]
    xn = _rms(x_ref[0], g_ref[...]).astype(BF16)
    out_refs = (o0_ref, o1_ref, o2_ref)
    for j, dil in enumerate(DILATIONS):
        t = _dot(xn, w_ref[:, j * D_MODEL:(j + 1) * D_MODEL])
        for p in range(D_MODEL // LANES):
            cols = slice(p * LANES, (p + 1) * LANES)
            tp = t[:, cols]
            if rope:
                tp = tp * cos_ref[...] + pltpu.roll(tp, LANES // 2, 1) * sin_ref[...]
            if dil == 1:
                out_refs[j][0, 0, :, cols] = tp.astype(BF16)
            else:
                t_ref[p] = tp
                for r in range(dil):
                    out_refs[j][0, r, :, cols] = t_ref[p, pl.ds(r, tm // dil, stride=dil), :].astype(BF16)


def _norm_proj(x, g, w, cos, sin, rope):
    B, S, D = x.shape
    tm = TOKEN_TILE
    return pl.pallas_call(
        functools.partial(_proj_kernel, rope=rope),
        grid=(B, S // tm),
        in_specs=[
            pl.BlockSpec((1, tm, D), lambda b, s: (b, s, 0)),
            _const_spec((1, D)),
            _const_spec((D, N_BRANCH * D)),
            pl.BlockSpec((tm, LANES), lambda b, s: (s, 0)),
            pl.BlockSpec((tm, LANES), lambda b, s: (s, 0)),
        ],
        out_specs=[pl.BlockSpec((1, dil, tm // dil, D), lambda b, s: (b, 0, s, 0)) for dil in DILATIONS],
        out_shape=[jax.ShapeDtypeStruct((B, dil, S // dil, D), BF16) for dil in DILATIONS],
        scratch_shapes=[pltpu.VMEM((D // LANES, tm, LANES), F32)],
        compiler_params=_params(("arbitrary", "arbitrary")),
        name="norm_proj_rope" if rope else "norm_proj",
    )(x, g, w, cos, sin)


def _rope_tables(seq_len, scale):
    half = ATT_HEAD_DIM // 2
    inv_freq = ROPE_THETA ** (-jnp.arange(half, dtype=F32) / half)
    ang = jnp.arange(seq_len, dtype=F32)[:, None] * inv_freq[None, :]
    cos = jnp.tile(jnp.cos(ang), (1, LANES // half))
    sin = jnp.tile(jnp.sin(ang), (1, LANES // half))
    sign = jnp.where(jnp.arange(LANES) < LANES // 2, -1.0, 1.0).astype(F32)
    return cos * scale, sin * sign[None, :] * scale


def _pair_block_perm():
    half = ATT_HEAD_DIM // 2
    idx = []
    for p in range(ATT_HEADS // 2):
        base = p * LANES
        for part in range(2):
            for hh in range(2):
                start = base + hh * ATT_HEAD_DIM + part * half
                idx.extend(range(start, start + half))
    return jnp.asarray(idx, dtype=jnp.int32)


def _attn_kernel(q_ref, kp_ref, kc_ref, vp_ref, vc_ref, o_ref, lse_ref):
    n = pl.program_id(2)
    qi = lax.broadcasted_iota(jnp.int32, (BAND, 2 * BAND), 0)
    ki = lax.broadcasted_iota(jnp.int32, (BAND, 2 * BAND), 1)
    dist = qi + BAND - ki
    first_key = jnp.where(n > 0, 0, BAND)
    valid = (dist >= 0) & (dist <= BAND) & (ki >= first_key)
    lane = lax.broadcasted_iota(jnp.int32, (BAND, LANES), 1)
    lse_tile = jnp.zeros((BAND, LANES), F32)
    for p in range(ATT_HEADS // 2):
        cols = slice(p * LANES, (p + 1) * LANES)
        qp = q_ref[:, cols]
        kcat = jnp.concatenate([kp_ref[:, cols], kc_ref[:, cols]], axis=0)
        vcat = jnp.concatenate([vp_ref[:, cols], vc_ref[:, cols]], axis=0)
        o_pair = jnp.zeros((BAND, LANES), F32)
        for a in range(2):
            qm = jnp.where((lane // (ATT_HEAD_DIM // 2)) % 2 == a, qp, jnp.zeros_like(qp))
            s = jnp.where(valid, _dot_nt(qm, kcat), NEG_BIG)
            m = jnp.max(s, axis=-1, keepdims=True)
            pr = jnp.exp2(s - m)
            l = jnp.sum(pr, axis=-1, keepdims=True)
            o = _dot(pr.astype(BF16), vcat) / l
            o_pair = jnp.where(lane // ATT_HEAD_DIM == a, o, o_pair)
            lse_tile = jnp.where(lane == 2 * p + a, m * LN2 + jnp.log(l), lse_tile)
        o_ref[:, cols] = o_pair.astype(o_ref.dtype)
    lse_ref[...] = lse_tile


def _dilated_attn(q, k, v, dil):
    B, _, L, D = q.shape
    nb = L // BAND
    blk = (None, None, BAND, D)
    cur = lambda b, r, n: (b, r, n, 0)
    prev = lambda b, r, n: (b, r, jnp.maximum(n - 1, 0), 0)
    return pl.pallas_call(
        _attn_kernel,
        grid=(B, dil, nb),
        in_specs=[
            pl.BlockSpec(blk, cur),
            pl.BlockSpec(blk, prev),
            pl.BlockSpec(blk, cur),
            pl.BlockSpec(blk, prev),
            pl.BlockSpec(blk, cur),
        ],
        out_specs=[pl.BlockSpec(blk, cur), pl.BlockSpec((None, None, BAND, LANES), cur)],
        out_shape=[
            jax.ShapeDtypeStruct((B, dil, L, D), BF16),
            jax.ShapeDtypeStruct((B, dil, L, LANES), F32),
        ],
        compiler_params=_params(("arbitrary", "arbitrary", "arbitrary")),
        name=f"dilated_attn_d{dil}",
    )(q, k, k, v, v)


def _combine_kernel(o0_ref, o1_ref, o2_ref, l0_ref, l1_ref, l2_ref, h_ref, wout_ref, expand_ref,
                    g_ref, wr_ref, h_out_ref, hn_out_ref, route_ref, o_scr, l_scr):
    tm = h_ref.shape[1]
    o_refs = (o0_ref, o1_ref, o2_ref)
    l_refs = (l0_ref, l1_ref, l2_ref)
    for i, dil in enumerate(DILATIONS):
        for r in range(dil):
            rows = pl.ds(r, tm // dil, stride=dil)
            l_scr[i, rows, :] = l_refs[i][0, r]
            for p in range(D_MODEL // LANES):
                o_scr[i, p, rows, :] = o_refs[i][0, r, :, p * LANES:(p + 1) * LANES].astype(F32)
    lses = [l_scr[i] for i in range(N_BRANCH)]
    m = jnp.maximum(jnp.maximum(lses[0], lses[1]), lses[2])
    es = [jnp.exp(l - m) for l in lses]
    den = es[0] + es[1] + es[2]
    o = None
    for i in range(N_BRANCH):
        w = es[i] / den
        w_hi = w.astype(BF16)
        w_lo = (w - w_hi.astype(F32)).astype(BF16)
        w_full = _dot(jnp.concatenate([w_hi, w_lo], axis=1), expand_ref[...])
        term = w_full * jnp.concatenate([o_scr[i, p] for p in range(D_MODEL // LANES)], axis=1)
        o = term if o is None else o + term
    h = h_ref[0] + _dot(o.astype(BF16), wout_ref[...])
    h_out_ref[...] = h
    hn = _rms(h, g_ref[...])
    hn_hi = hn.astype(BF16)
    hn_out_ref[...] = hn_hi

    hn_lo = (hn - hn_hi.astype(F32)).astype(BF16)
    l_hi = _dot(hn_hi, wr_ref[...])
    logits = l_hi[:, :LANES] + l_hi[:, LANES:] + _dot(hn_lo, wr_ref[:, :LANES])
    lane = lax.broadcasted_iota(jnp.int32, logits.shape, 1)
    lane_f = lane.astype(F32)
    logits = jnp.where(lane < N_EXPERTS, logits, NEG_BIG)
    mx = jnp.max(logits, axis=-1, keepdims=True)
    pe = jnp.exp(logits - mx)
    probs = pe / jnp.sum(pe, axis=-1, keepdims=True)
    probs = jnp.where(lane < N_EXPERTS, probs, -1.0)
    p1 = jnp.max(probs, axis=-1, keepdims=True)
    i1 = jnp.min(jnp.where(probs == p1, lane_f, float(LANES)), axis=-1, keepdims=True)
    rest = jnp.where(lane_f == i1, -1.0, probs)
    p2 = jnp.max(rest, axis=-1, keepdims=True)
    i2 = jnp.min(jnp.where(rest == p2, lane_f, float(LANES)), axis=-1, keepdims=True)
    tot = p1 + p2
    route = jnp.where(lane_f == i1, p1 / tot, 0.0)
    route = jnp.where(lane_f == i2, p2 / tot, route)
    sel = jnp.where((lane_f == i1 + N_EXPERTS) | (lane_f == i2 + N_EXPERTS), 1.0, 0.0)
    route_ref[...] = route + sel


def _attn_combine(os, lses, h, w_out, expand, g, wr):
    B, S, D = h.shape
    T = B * S
    tm = TOKEN_TILE
    ts = S // tm
    row = lambda b, s: (b * ts + s, 0)
    o_specs = [pl.BlockSpec((1, dil, tm // dil, D), lambda b, s: (b, 0, s, 0)) for dil in DILATIONS]
    l_specs = [pl.BlockSpec((1, dil, tm // dil, LANES), lambda b, s: (b, 0, s, 0)) for dil in DILATIONS]
    return pl.pallas_call(
        _combine_kernel,
        grid=(B, ts),
        in_specs=o_specs + l_specs + [
            pl.BlockSpec((1, tm, D), lambda b, s: (b, s, 0)),
            _const_spec((D, D)),
            _const_spec((2 * LANES, D)),
            _const_spec((1, D)),
            _const_spec((D, 2 * LANES)),
        ],
        out_specs=[pl.BlockSpec((tm, D), row), pl.BlockSpec((tm, D), row), pl.BlockSpec((tm, LANES), row)],
        out_shape=[
            jax.ShapeDtypeStruct((T, D), F32),
            jax.ShapeDtypeStruct((T, D), BF16),
            jax.ShapeDtypeStruct((T, LANES), F32),
        ],
        scratch_shapes=[pltpu.VMEM((N_BRANCH, D // LANES, tm, LANES), F32),
                        pltpu.VMEM((N_BRANCH, tm, LANES), F32)],
        compiler_params=_params(("arbitrary", "arbitrary")),
        name="attn_combine_route",
    )(*os, *lses, h, w_out, expand, g, wr)


def _slot_one_hot(route, n_slots, weighted=False):
    tm = route.shape[0]
    n_cols = N_EXPERTS * n_slots
    shift = n_slots.bit_length() - 1
    assert 1 << shift == n_slots
    lane = lax.broadcasted_iota(jnp.int32, route.shape, 1)
    sel = jnp.where((lane >= N_EXPERTS) & (lane < 2 * N_EXPERTS), route, 0.0)
    r = lax.broadcasted_iota(jnp.int32, (tm, tm), 0)
    c = lax.broadcasted_iota(jnp.int32, (tm, tm), 1)
    strict_lower = jnp.where(c < r, 1.0, 0.0).astype(BF16)
    rank = _dot(strict_lower, sel.astype(BF16))
    key = jnp.where(sel > 0.5, rank, -1.0).astype(BF16)
    src = lax.broadcasted_iota(jnp.int32, (LANES, n_cols), 0)
    col_expert = lax.shift_right_logical(lax.broadcasted_iota(jnp.int32, (LANES, n_cols), 1), shift)
    key_cols = _dot(key, jnp.where(src == col_expert + N_EXPERTS, 1.0, 0.0).astype(BF16))
    slot = lax.broadcasted_iota(jnp.int32, (tm, n_cols), 1) & (n_slots - 1)
    hit = key_cols == slot.astype(F32)
    if not weighted:
        return jnp.where(hit, 1.0, 0.0)
    w_hi = route.astype(BF16)
    w_lo = (route - w_hi.astype(F32)).astype(BF16)
    spread = jnp.where(src == col_expert, 1.0, 0.0).astype(BF16)
    w_cols = _dot(jnp.concatenate([w_hi, w_lo], axis=1), jnp.concatenate([spread, spread], axis=0))
    return jnp.where(hit, w_cols, 0.0)


def _block_copy(src, dst, sem):
    return pltpu.make_async_copy(src, dst, sem)


def _when_short(short_ref, step, fn, tm):
    @pl.when(short_ref[step] > 0)
    def _():
        fn(tm // 2)

    @pl.when(short_ref[step] == 0)
    def _():
        fn(tm)


def _dispatch_kernel(offs_ref, short_ref, fill_ref, route_ref, hn_ref, xs_hbm, stage, sem):
    j = pl.program_id(0)
    n = pl.num_programs(0)
    tm = hn_ref.shape[0]
    n_tiles = xs_hbm.shape[0] // tm
    slot = j % 2

    def zero_copy(row):
        return _block_copy(stage.at[1, 0], xs_hbm.at[pl.ds(pl.multiple_of(row, SUBLANES), tm)], sem.at[1])

    @pl.when(j == 0)
    def _():
        stage[1, 0] = jnp.zeros((tm, D_MODEL), F32)
        first_tail = fill_ref[N_EXPERTS]

        def start_tail(t, carry):
            zero_copy(t * tm).start()
            return carry

        def wait_tail(t, carry):
            zero_copy(t * tm).wait()
            return carry

        first = [zero_copy(fill_ref[e]) for e in range(N_EXPERTS)]
        second = [zero_copy(fill_ref[e] + tm) for e in range(N_EXPERTS)]
        for cp in first:
            cp.start()
        lax.fori_loop(first_tail, n_tiles, start_tail, 0)
        for cp in first:
            cp.wait()
        lax.fori_loop(first_tail, n_tiles, wait_tail, 0)
        for cp in second:
            cp.start()
        for cp in second:
            cp.wait()

    def copies(step_slot, step, n_rows):
        return [_block_copy(stage.at[step_slot, e, pl.ds(0, n_rows)],
                            xs_hbm.at[pl.ds(pl.multiple_of(offs_ref[step * N_EXPERTS + e], SUBLANES), n_rows)],
                            sem.at[step_slot])
                for e in range(N_EXPERTS)]

    def place(n_rows):
        one_hot = _slot_one_hot(route_ref[...], n_rows).astype(BF16)
        placed = _dot_tn(one_hot, hn_ref[...])
        for e in range(N_EXPERTS):
            stage[slot, e, :n_rows] = placed[e * n_rows:(e + 1) * n_rows]

    _when_short(short_ref, j, place, tm)

    @pl.when(j > 0)
    def _():
        _when_short(short_ref, j - 1, lambda n_rows: [cp.wait() for cp in copies(1 - slot, j - 1, n_rows)], tm)

    _when_short(short_ref, j, lambda n_rows: [cp.start() for cp in copies(slot, j, n_rows)], tm)

    @pl.when(j == n - 1)
    def _():
        _when_short(short_ref, j, lambda n_rows: [cp.wait() for cp in copies(slot, j, n_rows)], tm)


def _moe_dispatch(route, hn, offs, short, fill, n_rows):
    T, D = hn.shape
    tm = TOKEN_TILE
    grid_spec = pltpu.PrefetchScalarGridSpec(
        num_scalar_prefetch=3,
        grid=(T // tm,),
        in_specs=[
            pl.BlockSpec((tm, LANES), lambda j, *_: (j, 0)),
            pl.BlockSpec((tm, D), lambda j, *_: (j, 0)),
        ],
        out_specs=pl.BlockSpec(memory_space=pl.ANY),
        scratch_shapes=[pltpu.VMEM((2, N_EXPERTS, tm, D), F32), pltpu.SemaphoreType.DMA((2,))],
    )
    return pl.pallas_call(
        _dispatch_kernel,
        grid_spec=grid_spec,
        out_shape=jax.ShapeDtypeStruct((n_rows, D), F32),
        compiler_params=_params(("arbitrary",)),
        name="moe_dispatch",
    )(offs, short, fill, route, hn)


def _expert_kernel(te_ref, xb_ref, act_ref, x_ref, wg_ref, wu_ref, wd_ref, y_ref):
    del te_ref, xb_ref
    i = pl.program_id(0)
    tm = x_ref.shape[0]

    @pl.when(act_ref[i] > 0)
    def _():
        x = x_ref[...].astype(BF16)
        f_dim = wg_ref.shape[2]
        fc = 512
        acc = jnp.zeros((tm, D_MODEL), F32)
        for c in range(f_dim // fc):
            cs = slice(c * fc, (c + 1) * fc)
            a = _silu(_dot(x, wg_ref[0, :, cs])) * _dot(x, wu_ref[0, :, cs])
            acc = acc + _dot(a.astype(BF16), wd_ref[0, cs, :])
        y_ref[...] = acc

    @pl.when(act_ref[i] == 0)
    def _():
        y_ref[...] = jnp.zeros_like(y_ref)


def _moe_experts(xs, tile_expert, x_block, active, w_gate, w_up, w_down):
    n_rows, D = xs.shape
    Fd = w_gate.shape[2]
    tm = MOE_ROW_TILE
    G = n_rows // tm
    grid_spec = pltpu.PrefetchScalarGridSpec(
        num_scalar_prefetch=3,
        grid=(G,),
        in_specs=[
            pl.BlockSpec((tm, D), lambda i, te, xb, act: (xb[i], 0)),
            pl.BlockSpec((1, D, Fd), lambda i, te, xb, act: (te[i], 0, 0), pipeline_mode=pl.Buffered(1)),
            pl.BlockSpec((1, D, Fd), lambda i, te, xb, act: (te[i], 0, 0), pipeline_mode=pl.Buffered(1)),
            pl.BlockSpec((1, Fd, D), lambda i, te, xb, act: (te[i], 0, 0), pipeline_mode=pl.Buffered(1)),
        ],
        out_specs=pl.BlockSpec((tm, D), lambda i, te, xb, act: (i, 0)),
    )
    return pl.pallas_call(
        _expert_kernel,
        grid_spec=grid_spec,
        out_shape=jax.ShapeDtypeStruct((n_rows, D), F32),
        compiler_params=_params(("arbitrary",)),
        name="moe_experts",
    )(tile_expert, x_block, active, xs, w_gate, w_up, w_down)


def _finish_kernel(offs_ref, short_ref, route_ref, h_ref, g_ref, ys_hbm, out_ref, ybuf, sem):
    j = pl.program_id(0)
    n = pl.num_programs(0)
    tm = h_ref.shape[0]
    slot = j % 2

    def copies(step_slot, step, n_rows):
        return [_block_copy(ys_hbm.at[pl.ds(pl.multiple_of(offs_ref[step * N_EXPERTS + e], SUBLANES), n_rows)],
                            ybuf.at[step_slot, e, pl.ds(0, n_rows)], sem.at[step_slot])
                for e in range(N_EXPERTS)]

    @pl.when(j == 0)
    def _():
        _when_short(short_ref, 0, lambda n_rows: [cp.start() for cp in copies(0, 0, n_rows)], tm)

    @pl.when(j + 1 < n)
    def _():
        _when_short(short_ref, j + 1, lambda n_rows: [cp.start() for cp in copies(1 - slot, j + 1, n_rows)], tm)

    def gather(n_rows):
        route = route_ref[...]
        w = _slot_one_hot(route, n_rows, weighted=True)
        w_hi = w.astype(BF16)
        w_lo = (w - w_hi.astype(F32)).astype(BF16)
        for cp in copies(slot, j, n_rows):
            cp.wait()
        y = jnp.concatenate([ybuf[slot, e, :n_rows].astype(BF16) for e in range(N_EXPERTS)], axis=0)
        out_ref[...] = _rms(h_ref[...] + _dot(w_hi, y) + _dot(w_lo, y), g_ref[...])

    _when_short(short_ref, j, gather, tm)


def _moe_finish(ys, offs, short, route, h, g):
    T, D = h.shape
    tm = TOKEN_TILE
    grid_spec = pltpu.PrefetchScalarGridSpec(
        num_scalar_prefetch=2,
        grid=(T // tm,),
        in_specs=[
            pl.BlockSpec((tm, LANES), lambda j, *_: (j, 0)),
            pl.BlockSpec((tm, D), lambda j, *_: (j, 0)),
            pl.BlockSpec((1, D), lambda j, *_: (0, 0)),
            pl.BlockSpec(memory_space=pl.ANY),
        ],
        out_specs=pl.BlockSpec((tm, D), lambda j, *_: (j, 0)),
        scratch_shapes=[pltpu.VMEM((2, N_EXPERTS, tm, D), F32), pltpu.SemaphoreType.DMA((2,))],
    )
    return pl.pallas_call(
        _finish_kernel,
        grid_spec=grid_spec,
        out_shape=jax.ShapeDtypeStruct((T, D), F32),
        compiler_params=_params(("arbitrary",)),
        name="moe_finish",
    )(offs, short, route, h, g, ys)


def _route_metadata(route, tm):
    T = route.shape[0]
    nt = T // tm
    sel = (route[:, N_EXPERTS:2 * N_EXPERTS] > 0.5).astype(jnp.int32)
    cnt_tile = sel.reshape(nt, tm, N_EXPERTS).sum(axis=1)
    short = (jnp.max(cnt_tile, axis=1) <= tm // 2).astype(jnp.int32)
    cnt_tile = ((cnt_tile + SUBLANES - 1) // SUBLANES) * SUBLANES
    n_e = cnt_tile.sum(axis=0)
    padded = ((n_e + tm - 1) // tm + 1) * tm
    ends = jnp.cumsum(padded)
    starts = ends - padded
    offs = starts[None, :] + jnp.cumsum(cnt_tile, axis=0) - cnt_tile
    fill = starts + n_e
    max_rows = TOP_K * T + nt * N_EXPERTS * (SUBLANES - 1)
    G = -(-max_rows // tm) + 2 * N_EXPERTS + 1
    tile_start = jnp.arange(G, dtype=jnp.int32) * tm
    tile_expert = jnp.minimum(jnp.sum((tile_start[:, None] >= ends[None, :]).astype(jnp.int32), axis=1),
                              N_EXPERTS - 1)
    active = (tile_start < fill[tile_expert]).astype(jnp.int32)
    x_block = jnp.where(active > 0, jnp.arange(G, dtype=jnp.int32), jnp.argmax(active).astype(jnp.int32))
    fill = jnp.concatenate([fill, ends[-1:] // tm])
    return (offs.reshape(-1).astype(jnp.int32), short, fill.astype(jnp.int32), tile_expert.astype(jnp.int32),
            x_block, active, G * tm)


def kernel(x, attn_norm, ffn_norm, hgrn_w_in, hgrn_lower_bounds, hgrn_out_norm, hgrn_w_out, kv_norm, w_kv,
           dil_w_q, dil_w_out, ffn_w_gate, ffn_w_up, ffn_w_down, moe_w_router, moe_w_gate, moe_w_up,
           moe_w_down, final_norm):
    B, S, D = x.shape
    row = lambda v: v.reshape(1, -1).astype(F32)

    lbs = jnp.cumsum(jax.nn.softmax(hgrn_lower_bounds.astype(F32), axis=0), axis=0)
    h = _hgrn_mixer(x, row(attn_norm[0]), hgrn_w_in[0].astype(BF16), row(lbs[0]), row(hgrn_out_norm[0]),
                    hgrn_w_out[0].astype(BF16))
    h = _dense_ffn(h, row(ffn_norm[0]), ffn_w_gate[0].astype(BF16), ffn_w_up[0].astype(BF16),
                   ffn_w_down[0].astype(BF16))

    perm = _pair_block_perm()
    perm3 = jnp.concatenate([perm + i * D for i in range(N_BRANCH)])
    cos_k, sin_k = _rope_tables(S, 1.0)
    cos_q, sin_q = _rope_tables(S, ATT_SCALE * LOG2E)
    w_k = w_kv[:, :N_BRANCH * D][:, perm3].astype(BF16)
    w_v = w_kv[:, N_BRANCH * D:].astype(BF16)
    w_q = dil_w_q[0][:, perm3].astype(BF16)
    ks = _norm_proj(h, row(kv_norm), w_k, cos_k, sin_k, rope=True)
    vs = _norm_proj(h, row(kv_norm), w_v, cos_k, sin_k, rope=False)
    qs = _norm_proj(h, row(attn_norm[1]), w_q, cos_q, sin_q, rope=True)
    os, lses = [], []
    for i, dil in enumerate(DILATIONS):
        o_i, lse_i = _dilated_attn(qs[i], ks[i], vs[i], dil)
        os.append(o_i)
        lses.append(lse_i)

    head_of_col = jnp.arange(D, dtype=jnp.int32) // ATT_HEAD_DIM
    expand = (jnp.arange(LANES, dtype=jnp.int32)[:, None] == head_of_col[None, :]).astype(BF16)
    expand = jnp.concatenate([expand, expand], axis=0)
    wr = jnp.zeros((D, LANES), F32).at[:, :N_EXPERTS].set(moe_w_router[0].astype(F32))
    wr_hi = wr.astype(BF16)
    wr_lo = (wr - wr_hi.astype(F32)).astype(BF16)
    h, hn, route = _attn_combine(os, lses, h, dil_w_out[0].astype(BF16), expand, row(ffn_norm[1]),
                                 jnp.concatenate([wr_hi, wr_lo], axis=1))

    offs, short, fill, tile_expert, x_block, active, n_rows = _route_metadata(route, TOKEN_TILE)
    xs = _moe_dispatch(route, hn, offs, short, fill, n_rows)
    ys = _moe_experts(xs, tile_expert, x_block, active, moe_w_gate[0].astype(BF16), moe_w_up[0].astype(BF16),
                      moe_w_down[0].astype(BF16))
    out = _moe_finish(ys, offs, short, route, h, row(final_norm))
    return out.reshape(B, S, D)
```

```python
import functools

import jax
import jax.numpy as jnp
from jax import lax
from jax.experimental import pallas as pl
from jax.experimental.pallas import tpu as pltpu

F32 = jnp.float32
BF16 = jnp.bfloat16

D_MODEL = 1024
HGRN_HEAD_DIM = 128
HGRN_HEADS = D_MODEL // HGRN_HEAD_DIM
HGRN_CHUNK = 64
ATT_HEAD_DIM = 64
ATT_HEADS = D_MODEL // ATT_HEAD_DIM
DILATIONS = (1, 4, 16)
N_BRANCH = len(DILATIONS)
BAND = 128
ATT_SCALE = ATT_HEAD_DIM ** -0.5
ROPE_THETA = 10000.0
N_EXPERTS = 8
TOP_K = 2
EPS = 1e-6

LANES = 128
SUBLANES = 8
TOKEN_TILE = 256
WIDE_TILE = 512
MOE_ROW_TILE = TOKEN_TILE
VMEM_LIMIT = 56 * 1024 * 1024
NEG_BIG = -1e30
LN2 = 0.6931471805599453
LOG2E = 1.0 / LN2


def _dot(a, b):
    return jnp.dot(a, b, preferred_element_type=F32)


def _dot_nt(a, b):
    return lax.dot_general(a, b, (((1,), (1,)), ((), ())), preferred_element_type=F32)


def _dot_tn(a, b):
    return lax.dot_general(a, b, (((0,), (0,)), ((), ())), preferred_element_type=F32)


def _rms(x, g):
    return x * lax.rsqrt(jnp.mean(x * x, axis=-1, keepdims=True) + EPS) * g


def _sigmoid(x):
    return 1.0 / (1.0 + jnp.exp(-x))


def _silu(x):
    return x * _sigmoid(x)


def _split3(x):
    hi = x.astype(BF16)
    r1 = x - hi.astype(F32)
    mid = r1.astype(BF16)
    lo = (r1 - mid.astype(F32)).astype(BF16)
    return hi, mid, lo


def _const_spec(shape):
    nd = len(shape)
    return pl.BlockSpec(shape, lambda *_: (0,) * nd, pipeline_mode=pl.Buffered(1))


def _params(sem):
    return pltpu.CompilerParams(dimension_semantics=sem, vmem_limit_bytes=VMEM_LIMIT)


def _hgrn_kernel(x_ref, g_ref, win_ref, lb_ref, onorm_ref, wout_ref, out_ref,
                 state_ref, y_ref, o_ref, k_ref, b_ref, oi_ref, s_ref):
    tm = x_ref.shape[1]
    dk = HGRN_HEAD_DIM
    c_len = HGRN_CHUNK

    @pl.when(pl.program_id(1) == 0)
    def _():
        state_ref[...] = jnp.zeros_like(state_ref)

    x = x_ref[0]
    xn = _rms(x, g_ref[...]).astype(BF16)
    y_ref[...] = _dot(xn, win_ref[...])

    r = lax.broadcasted_iota(jnp.int32, (tm, tm), 0)
    c = lax.broadcasted_iota(jnp.int32, (tm, tm), 1)
    tri = jnp.where(c <= r, 1.0, 0.0).astype(BF16)

    lb = lb_ref[...]
    f = lb + (1.0 - lb) * _sigmoid(y_ref[:, D_MODEL:2 * D_MODEL])
    k_ref[...] = 1.0 - f
    lf_hi, lf_mid, lf_lo = _split3(jnp.log(f))
    b_ref[...] = _dot(tri, lf_hi) + _dot(tri, lf_mid) + _dot(tri, lf_lo)

    for h in range(HGRN_HEADS):
        cols = slice(h * dk, (h + 1) * dk)
        q = y_ref[:, h * dk:(h + 1) * dk]
        v16 = y_ref[:, 2 * D_MODEL + h * dk:2 * D_MODEL + (h + 1) * dk].astype(BF16)
        k = k_ref[:, cols]
        b = b_ref[:, cols]
        b_last = b[tm - 1:tm, :]
        st = state_ref[h]
        oi_ref[:, cols] = _dot_nt((q * jnp.exp(b)).astype(BF16), st.astype(BF16))
        k_end = (k * jnp.exp(b_last - b)).astype(BF16)
        state_ref[h] = st * jnp.exp(b_last) + _dot_tn(v16, k_end)
        for ci in range(tm // c_len):
            rows = slice(ci * c_len, (ci + 1) * c_len)
            n_keys = (ci + 1) * c_len
            ref = b[ci * c_len - 1:ci * c_len, :] if ci > 0 else jnp.zeros((1, dk), F32)
            q_dec = (q[rows] * jnp.exp(b[rows] - ref)).astype(BF16)
            k_dec = (k[:n_keys] * jnp.exp(ref - b[:n_keys])).astype(BF16)
            qpos = lax.broadcasted_iota(jnp.int32, (c_len, n_keys), 0) + ci * c_len
            kpos = lax.broadcasted_iota(jnp.int32, (c_len, n_keys), 1)
            s_ref[h, rows, :n_keys] = jnp.where(kpos <= qpos, _dot_nt(q_dec, k_dec), 0.0).astype(BF16)
            if n_keys < tm:
                s_ref[h, rows, n_keys:] = jnp.zeros((c_len, tm - n_keys), BF16)

    for h in range(HGRN_HEADS):
        cols = slice(h * dk, (h + 1) * dk)
        v16 = y_ref[:, 2 * D_MODEL + h * dk:2 * D_MODEL + (h + 1) * dk].astype(BF16)
        gate = y_ref[:, 3 * D_MODEL + h * dk:3 * D_MODEL + (h + 1) * dk]
        o = _dot(s_ref[h], v16) + oi_ref[:, cols]
        o_ref[:, cols] = (_rms(o, onorm_ref[...]) * _silu(gate)).astype(BF16)

    out_ref[0] = x + _dot(o_ref[...], wout_ref[...])


def _hgrn_mixer(x, g, w_in, lb, out_norm, w_out):
    B, S, D = x.shape
    tm = TOKEN_TILE
    return pl.pallas_call(
        _hgrn_kernel,
        grid=(B, S // tm),
        in_specs=[
            pl.BlockSpec((1, tm, D), lambda b, s: (b, s, 0)),
            _const_spec((1, D)),
            _const_spec((D, 4 * D)),
            _const_spec((1, D)),
            _const_spec((1, HGRN_HEAD_DIM)),
            _const_spec((D, D)),
        ],
        out_specs=pl.BlockSpec((1, tm, D), lambda b, s: (b, s, 0)),
        out_shape=jax.ShapeDtypeStruct((B, S, D), F32),
        scratch_shapes=[
            pltpu.VMEM((HGRN_HEADS, HGRN_HEAD_DIM, HGRN_HEAD_DIM), F32),
            pltpu.VMEM((tm, 4 * D), F32),
            pltpu.VMEM((tm, D), BF16),
            pltpu.VMEM((tm, D), F32),
            pltpu.VMEM((tm, D), F32),
            pltpu.VMEM((tm, D), F32),
            pltpu.VMEM((HGRN_HEADS, tm, tm), BF16),
        ],
        compiler_params=_params(("arbitrary", "arbitrary")),
        name="hgrn_mixer",
    )(x, g, w_in, lb, out_norm, w_out)


def _ffn_kernel(x_ref, g_ref, wg_ref, wu_ref, wd_ref, out_ref):
    x = x_ref[0]
    xn = _rms(x, g_ref[...]).astype(BF16)
    a = _silu(_dot(xn, wg_ref[...])) * _dot(xn, wu_ref[...])
    out_ref[0] = x + _dot(a.astype(BF16), wd_ref[...])


def _dense_ffn(x, g, w_gate, w_up, w_down):
    B, S, D = x.shape
    Fd = w_gate.shape[1]
    tm = WIDE_TILE
    return pl.pallas_call(
        _ffn_kernel,
        grid=(B, S // tm),
        in_specs=[
            pl.BlockSpec((1, tm, D), lambda b, s: (b, s, 0)),
            _const_spec((1, D)),
            _const_spec((D, Fd)),
            _const_spec((D, Fd)),
            _const_spec((Fd, D)),
        ],
        out_specs=pl.BlockSpec((1, tm, D), lambda b, s: (b, s, 0)),
        out_shape=jax.ShapeDtypeStruct((B, S, D), F32),
        compiler_params=_params(("arbitrary", "arbitrary")),
        name="dense_ffn",
    )(x, g, w_gate, w_up, w_down)


def _proj_kernel(x_ref, g_ref, w_ref, cos_ref, sin_ref, o0_ref, o1_ref, o2_ref, t_ref, *, rope):
    tm = x_ref.shape[1]
    xn = _rms(x_ref[0], g_ref[...]).astype(BF16)
    out_refs = (o0_ref, o1_ref, o2_ref)
    for j, dil in enumerate(DILATIONS):
        t = _dot(xn, w_ref[:, j * D_MODEL:(j + 1) * D_MODEL])
        for p in range(D_MODEL // LANES):
            cols = slice(p * LANES, (p + 1) * LANES)
            tp = t[:, cols]
            if rope:
                tp = tp * cos_ref[...] + pltpu.roll(tp, LANES // 2, 1) * sin_ref[...]
            if dil == 1:
                out_refs[j][0, 0, :, cols] = tp.astype(BF16)
            else:
                t_ref[p] = tp
                for r in range(dil):
                    out_refs[j][0, r, :, cols] = t_ref[p, pl.ds(r, tm // dil, stride=dil), :].astype(BF16)


def _norm_proj(x, g, w, cos, sin, rope):
    B, S, D = x.shape
    tm = WIDE_TILE
    return pl.pallas_call(
        functools.partial(_proj_kernel, rope=rope),
        grid=(B, S // tm),
        in_specs=[
            pl.BlockSpec((1, tm, D), lambda b, s: (b, s, 0)),
            _const_spec((1, D)),
            _const_spec((D, N_BRANCH * D)),
            pl.BlockSpec((tm, LANES), lambda b, s: (s, 0)),
            pl.BlockSpec((tm, LANES), lambda b, s: (s, 0)),
        ],
        out_specs=[pl.BlockSpec((1, dil, tm // dil, D), lambda b, s: (b, 0, s, 0)) for dil in DILATIONS],
        out_shape=[jax.ShapeDtypeStruct((B, dil, S // dil, D), BF16) for dil in DILATIONS],
        scratch_shapes=[pltpu.VMEM((D // LANES, tm, LANES), F32)],
        compiler_params=_params(("arbitrary", "arbitrary")),
        name="norm_proj_rope" if rope else "norm_proj",
    )(x, g, w, cos, sin)


def _rope_tables(seq_len, scale):
    half = ATT_HEAD_DIM // 2
    inv_freq = ROPE_THETA ** (-jnp.arange(half, dtype=F32) / half)
    ang = jnp.arange(seq_len, dtype=F32)[:, None] * inv_freq[None, :]
    cos = jnp.tile(jnp.cos(ang), (1, LANES // half))
    sin = jnp.tile(jnp.sin(ang), (1, LANES // half))
    sign = jnp.where(jnp.arange(LANES) < LANES // 2, -1.0, 1.0).astype(F32)
    return cos * scale, sin * sign[None, :] * scale


def _pair_block_columns(w):
    rows = w.shape[0]
    half = ATT_HEAD_DIM // 2
    w = w.reshape(rows, -1, 2, 2, half)
    return jnp.swapaxes(w, 2, 3).reshape(rows, -1).astype(BF16)


def _attn_kernel(q_ref, kp_ref, kc_ref, vp_ref, vc_ref, o_ref, lse_ref):
    n = pl.program_id(2)
    qi = lax.broadcasted_iota(jnp.int32, (BAND, 2 * BAND), 0)
    ki = lax.broadcasted_iota(jnp.int32, (BAND, 2 * BAND), 1)
    dist = qi + BAND - ki
    first_key = jnp.where(n > 0, 0, BAND)
    valid = (dist >= 0) & (dist <= BAND) & (ki >= first_key)
    lane = lax.broadcasted_iota(jnp.int32, (BAND, LANES), 1)
    m_tile = jnp.zeros((BAND, LANES), F32)
    l_tile = jnp.ones((BAND, LANES), F32)
    for p in range(ATT_HEADS // 2):
        cols = slice(p * LANES, (p + 1) * LANES)
        qp = q_ref[:, cols]
        kcat = jnp.concatenate([kp_ref[:, cols], kc_ref[:, cols]], axis=0)
        vcat = jnp.concatenate([vp_ref[:, cols], vc_ref[:, cols]], axis=0)
        o_pair = jnp.zeros((BAND, LANES), F32)
        for a in range(2):
            qm = jnp.where((lane // (ATT_HEAD_DIM // 2)) % 2 == a, qp, jnp.zeros_like(qp))
            s = jnp.where(valid, _dot_nt(qm, kcat), NEG_BIG)
            m = jnp.max(s, axis=-1, keepdims=True)
            pr = jnp.exp2(s - m)
            l = jnp.sum(pr, axis=-1, keepdims=True)
            o = _dot(pr.astype(BF16), vcat) / l
            o_pair = jnp.where(lane // ATT_HEAD_DIM == a, o, o_pair)
            m_tile = jnp.where(lane == 2 * p + a, m, m_tile)
            l_tile = jnp.where(lane == 2 * p + a, l, l_tile)
        o_ref[:, cols] = o_pair.astype(o_ref.dtype)
    lse_ref[...] = m_tile * LN2 + jnp.log(l_tile)


def _dilated_attn(q, k, v, dil):
    B, _, L, D = q.shape
    nb = L // BAND
    blk = (None, None, BAND, D)
    cur = lambda b, r, n: (b, r, n, 0)
    prev = lambda b, r, n: (b, r, jnp.maximum(n - 1, 0), 0)
    return pl.pallas_call(
        _attn_kernel,
        grid=(B, dil, nb),
        in_specs=[
            pl.BlockSpec(blk, cur),
            pl.BlockSpec(blk, prev),
            pl.BlockSpec(blk, cur),
            pl.BlockSpec(blk, prev),
            pl.BlockSpec(blk, cur),
        ],
        out_specs=[pl.BlockSpec(blk, cur), pl.BlockSpec((None, None, BAND, LANES), cur)],
        out_shape=[
            jax.ShapeDtypeStruct((B, dil, L, D), BF16),
            jax.ShapeDtypeStruct((B, dil, L, LANES), F32),
        ],
        compiler_params=_params(("arbitrary", "arbitrary", "arbitrary")),
        name=f"dilated_attn_d{dil}",
    )(q, k, k, v, v)


def _combine_kernel(o0_ref, o1_ref, o2_ref, l0_ref, l1_ref, l2_ref, h_ref, wout_ref, expand_ref,
                    g_ref, wr_ref, h_out_ref, hn_out_ref, route_ref, o_scr, l_scr):
    tm = h_ref.shape[1]
    o_refs = (o0_ref, o1_ref, o2_ref)
    l_refs = (l0_ref, l1_ref, l2_ref)
    for i, dil in enumerate(DILATIONS):
        for r in range(dil):
            rows = pl.ds(r, tm // dil, stride=dil)
            l_scr[i, rows, :] = l_refs[i][0, r]
            for p in range(D_MODEL // LANES):
                o_scr[i, p, rows, :] = o_refs[i][0, r, :, p * LANES:(p + 1) * LANES].astype(F32)
    lses = [l_scr[i] for i in range(N_BRANCH)]
    m = jnp.maximum(jnp.maximum(lses[0], lses[1]), lses[2])
    es = [jnp.exp(l - m) for l in lses]
    den = es[0] + es[1] + es[2]
    o = None
    for i in range(N_BRANCH):
        w = es[i] / den
        w_hi = w.astype(BF16)
        w_lo = (w - w_hi.astype(F32)).astype(BF16)
        w_full = _dot(jnp.concatenate([w_hi, w_lo], axis=1), expand_ref[...])
        term = w_full * jnp.concatenate([o_scr[i, p] for p in range(D_MODEL // LANES)], axis=1)
        o = term if o is None else o + term
    h = h_ref[0] + _dot(o.astype(BF16), wout_ref[...])
    h_out_ref[...] = h
    hn = _rms(h, g_ref[...])
    hn_hi = hn.astype(BF16)
    hn_out_ref[...] = hn_hi

    hn_lo = (hn - hn_hi.astype(F32)).astype(BF16)
    l_hi = _dot(hn_hi, wr_ref[...])
    logits = l_hi[:, :LANES] + l_hi[:, LANES:] + _dot(hn_lo, wr_ref[:, :LANES])
    lane = lax.broadcasted_iota(jnp.int32, logits.shape, 1)
    lane_f = lane.astype(F32)
    logits = jnp.where(lane < N_EXPERTS, logits, NEG_BIG)
    mx = jnp.max(logits, axis=-1, keepdims=True)
    pe = jnp.exp(logits - mx)
    probs = pe / jnp.sum(pe, axis=-1, keepdims=True)
    probs = jnp.where(lane < N_EXPERTS, probs, -1.0)
    p1 = jnp.max(probs, axis=-1, keepdims=True)
    i1 = jnp.min(jnp.where(probs == p1, lane_f, float(LANES)), axis=-1, keepdims=True)
    rest = jnp.where(lane_f == i1, -1.0, probs)
    p2 = jnp.max(rest, axis=-1, keepdims=True)
    i2 = jnp.min(jnp.where(rest == p2, lane_f, float(LANES)), axis=-1, keepdims=True)
    tot = p1 + p2
    route = jnp.where(lane_f == i1, p1 / tot, 0.0)
    route = jnp.where(lane_f == i2, p2 / tot, route)
    sel = jnp.where((lane_f == i1 + N_EXPERTS) | (lane_f == i2 + N_EXPERTS), 1.0, 0.0)
    route_ref[...] = route + sel


def _attn_combine(os, lses, h, w_out, expand, g, wr):
    B, S, D = h.shape
    T = B * S
    tm = WIDE_TILE
    ts = S // tm
    row = lambda b, s: (b * ts + s, 0)
    o_specs = [pl.BlockSpec((1, dil, tm // dil, D), lambda b, s: (b, 0, s, 0)) for dil in DILATIONS]
    l_specs = [pl.BlockSpec((1, dil, tm // dil, LANES), lambda b, s: (b, 0, s, 0)) for dil in DILATIONS]
    return pl.pallas_call(
        _combine_kernel,
        grid=(B, ts),
        in_specs=o_specs + l_specs + [
            pl.BlockSpec((1, tm, D), lambda b, s: (b, s, 0)),
            _const_spec((D, D)),
            _const_spec((2 * LANES, D)),
            _const_spec((1, D)),
            _const_spec((D, 2 * LANES)),
        ],
        out_specs=[pl.BlockSpec((tm, D), row), pl.BlockSpec((tm, D), row), pl.BlockSpec((tm, LANES), row)],
        out_shape=[
            jax.ShapeDtypeStruct((T, D), F32),
            jax.ShapeDtypeStruct((T, D), BF16),
            jax.ShapeDtypeStruct((T, LANES), F32),
        ],
        scratch_shapes=[pltpu.VMEM((N_BRANCH, D // LANES, tm, LANES), F32),
                        pltpu.VMEM((N_BRANCH, tm, LANES), F32)],
        compiler_params=_params(("arbitrary", "arbitrary")),
        name="attn_combine_route",
    )(*os, *lses, h, w_out, expand, g, wr)


SHORT_BLOCK = 96


def _spread_tables(n_slots):
    n_cols = N_EXPERTS * n_slots
    col_expert = jnp.arange(n_cols, dtype=jnp.int32) // n_slots
    src = jnp.arange(LANES, dtype=jnp.int32)
    spread = (src[:, None] == col_expert[None, :] + N_EXPERTS).astype(BF16)
    slots = (jnp.arange(n_cols, dtype=jnp.int32) % n_slots).astype(F32).reshape(1, n_cols)
    return jnp.concatenate([spread, spread], axis=0), slots


def _slot_one_hot(route, spread_ref, slots_ref, weighted=False):
    tm = route.shape[0]
    lane = lax.broadcasted_iota(jnp.int32, route.shape, 1)
    sel = jnp.where((lane >= N_EXPERTS) & (lane < 2 * N_EXPERTS), route, 0.0)
    r = lax.broadcasted_iota(jnp.int32, (tm, tm), 0)
    c = lax.broadcasted_iota(jnp.int32, (tm, tm), 1)
    strict_lower = jnp.where(c < r, 1.0, 0.0).astype(BF16)
    rank = _dot(strict_lower, sel.astype(BF16))
    key = jnp.where(sel > 0.5, rank, -1.0).astype(BF16)
    hit = _dot(key, spread_ref[:LANES, :]) == slots_ref[...]
    if not weighted:
        return jnp.where(hit, 1.0, 0.0)
    w = pltpu.roll(route, N_EXPERTS, 1)
    w_hi = w.astype(BF16)
    w_lo = (w - w_hi.astype(F32)).astype(BF16)
    return jnp.where(hit, _dot(jnp.concatenate([w_hi, w_lo], axis=1), spread_ref[...]), 0.0)


def _block_copy(src, dst, sem):
    return pltpu.make_async_copy(src, dst, sem)


def _when_short(short_ref, step, fn, tables, tm):
    @pl.when(short_ref[step] > 0)
    def _():
        fn(SHORT_BLOCK, *tables[:2])

    @pl.when(short_ref[step] == 0)
    def _():
        fn(tm, *tables[2:])


def _dispatch_kernel(offs_ref, short_ref, fill_ref, route_ref, hn_ref, sp_s, sl_s, sp_l, sl_l, xs_hbm, stage, sem):
    tables = (sp_s, sl_s, sp_l, sl_l)
    j = pl.program_id(0)
    n = pl.num_programs(0)
    tm = hn_ref.shape[0]
    n_tiles = xs_hbm.shape[0] // tm
    slot = j % 2

    def zero_copy(row):
        return _block_copy(stage.at[1, 0], xs_hbm.at[pl.ds(pl.multiple_of(row, SUBLANES), tm)], sem.at[1])

    @pl.when(j == 0)
    def _():
        stage[1, 0] = jnp.zeros((tm, D_MODEL), F32)
        first_tail = fill_ref[N_EXPERTS]

        def start_tail(t, carry):
            zero_copy(t * tm).start()
            return carry

        def wait_tail(t, carry):
            zero_copy(t * tm).wait()
            return carry

        first = [zero_copy(fill_ref[e]) for e in range(N_EXPERTS)]
        second = [zero_copy(fill_ref[e] + tm) for e in range(N_EXPERTS)]
        for cp in first:
            cp.start()
        lax.fori_loop(first_tail, n_tiles, start_tail, 0)
        for cp in first:
            cp.wait()
        lax.fori_loop(first_tail, n_tiles, wait_tail, 0)
        for cp in second:
            cp.start()
        for cp in second:
            cp.wait()

    def copies(step_slot, step, n_rows):
        return [_block_copy(stage.at[step_slot, e, pl.ds(0, n_rows)],
                            xs_hbm.at[pl.ds(pl.multiple_of(offs_ref[step * N_EXPERTS + e], SUBLANES), n_rows)],
                            sem.at[step_slot])
                for e in range(N_EXPERTS)]

    def place(n_rows, spread_ref, slots_ref):
        one_hot = _slot_one_hot(route_ref[...], spread_ref, slots_ref).astype(BF16)
        placed = _dot_tn(one_hot, hn_ref[...])
        for e in range(N_EXPERTS):
            stage[slot, e, :n_rows] = placed[e * n_rows:(e + 1) * n_rows]

    _when_short(short_ref, j, place, tables, tm)

    @pl.when(j > 0)
    def _():
        _when_short(short_ref, j - 1, lambda n_rows, *_: [cp.wait() for cp in copies(1 - slot, j - 1, n_rows)],
                    tables, tm)

    _when_short(short_ref, j, lambda n_rows, *_: [cp.start() for cp in copies(slot, j, n_rows)], tables, tm)

    @pl.when(j == n - 1)
    def _():
        _when_short(short_ref, j, lambda n_rows, *_: [cp.wait() for cp in copies(slot, j, n_rows)], tables, tm)


def _table_specs(tm):
    return [_const_spec((2 * LANES, N_EXPERTS * SHORT_BLOCK)), _const_spec((1, N_EXPERTS * SHORT_BLOCK)),
            _const_spec((2 * LANES, N_EXPERTS * tm)), _const_spec((1, N_EXPERTS * tm))]


def _moe_dispatch(route, hn, offs, short, fill, tables, n_rows):
    T, D = hn.shape
    tm = TOKEN_TILE
    grid_spec = pltpu.PrefetchScalarGridSpec(
        num_scalar_prefetch=3,
        grid=(T // tm,),
        in_specs=[
            pl.BlockSpec((tm, LANES), lambda j, *_: (j, 0)),
            pl.BlockSpec((tm, D), lambda j, *_: (j, 0)),
        ] + _table_specs(tm),
        out_specs=pl.BlockSpec(memory_space=pl.ANY),
        scratch_shapes=[pltpu.VMEM((2, N_EXPERTS, tm, D), F32), pltpu.SemaphoreType.DMA((2,))],
    )
    return pl.pallas_call(
        _dispatch_kernel,
        grid_spec=grid_spec,
        out_shape=jax.ShapeDtypeStruct((n_rows, D), F32),
        compiler_params=_params(("arbitrary",)),
        name="moe_dispatch",
    )(offs, short, fill, route, hn, *tables)


def _expert_kernel(te_ref, xb_ref, act_ref, x_ref, wg_ref, wu_ref, wd_ref, y_ref):
    del te_ref, xb_ref
    i = pl.program_id(0)
    tm = x_ref.shape[0]

    @pl.when(act_ref[i] > 0)
    def _():
        x = x_ref[...].astype(BF16)
        f_dim = wg_ref.shape[2]
        fc = 512
        acc = jnp.zeros((tm, D_MODEL), F32)
        for c in range(f_dim // fc):
            cs = slice(c * fc, (c + 1) * fc)
            a = _silu(_dot(x, wg_ref[0, :, cs])) * _dot(x, wu_ref[0, :, cs])
            acc = acc + _dot(a.astype(BF16), wd_ref[0, cs, :])
        y_ref[...] = acc

    @pl.when(act_ref[i] == 0)
    def _():
        y_ref[...] = jnp.zeros_like(y_ref)


def _moe_experts(xs, tile_expert, x_block, active, w_gate, w_up, w_down):
    n_rows, D = xs.shape
    Fd = w_gate.shape[2]
    tm = MOE_ROW_TILE
    G = n_rows // tm
    grid_spec = pltpu.PrefetchScalarGridSpec(
        num_scalar_prefetch=3,
        grid=(G,),
        in_specs=[
            pl.BlockSpec((tm, D), lambda i, te, xb, act: (xb[i], 0)),
            pl.BlockSpec((1, D, Fd), lambda i, te, xb, act: (te[i], 0, 0), pipeline_mode=pl.Buffered(1)),
            pl.BlockSpec((1, D, Fd), lambda i, te, xb, act: (te[i], 0, 0), pipeline_mode=pl.Buffered(1)),
            pl.BlockSpec((1, Fd, D), lambda i, te, xb, act: (te[i], 0, 0), pipeline_mode=pl.Buffered(1)),
        ],
        out_specs=pl.BlockSpec((tm, D), lambda i, te, xb, act: (i, 0)),
    )
    return pl.pallas_call(
        _expert_kernel,
        grid_spec=grid_spec,
        out_shape=jax.ShapeDtypeStruct((n_rows, D), F32),
        compiler_params=_params(("arbitrary",)),
        name="moe_experts",
    )(tile_expert, x_block, active, xs, w_gate, w_up, w_down)


def _finish_kernel(offs_ref, short_ref, route_ref, h_ref, g_ref, sp_s, sl_s, sp_l, sl_l, ys_hbm, out_ref, ybuf, sem):
    tables = (sp_s, sl_s, sp_l, sl_l)
    j = pl.program_id(0)
    n = pl.num_programs(0)
    tm = h_ref.shape[0]
    slot = j % 2

    def copies(step_slot, step, n_rows):
        return [_block_copy(ys_hbm.at[pl.ds(pl.multiple_of(offs_ref[step * N_EXPERTS + e], SUBLANES), n_rows)],
                            ybuf.at[step_slot, e, pl.ds(0, n_rows)], sem.at[step_slot])
                for e in range(N_EXPERTS)]

    @pl.when(j == 0)
    def _():
        _when_short(short_ref, 0, lambda n_rows, *_: [cp.start() for cp in copies(0, 0, n_rows)], tables, tm)

    @pl.when(j + 1 < n)
    def _():
        _when_short(short_ref, j + 1, lambda n_rows, *_: [cp.start() for cp in copies(1 - slot, j + 1, n_rows)],
                    tables, tm)

    def gather(n_rows, spread_ref, slots_ref):
        w = _slot_one_hot(route_ref[...], spread_ref, slots_ref, weighted=True)
        w_hi = w.astype(BF16)
        w_lo = (w - w_hi.astype(F32)).astype(BF16)
        for cp in copies(slot, j, n_rows):
            cp.wait()
        y = jnp.concatenate([ybuf[slot, e, :n_rows].astype(BF16) for e in range(N_EXPERTS)], axis=0)
        out_ref[...] = _rms(h_ref[...] + _dot(w_hi, y) + _dot(w_lo, y), g_ref[...])

    _when_short(short_ref, j, gather, tables, tm)


def _moe_finish(ys, offs, short, route, h, g, tables):
    T, D = h.shape
    tm = TOKEN_TILE
    grid_spec = pltpu.PrefetchScalarGridSpec(
        num_scalar_prefetch=2,
        grid=(T // tm,),
        in_specs=[
            pl.BlockSpec((tm, LANES), lambda j, *_: (j, 0)),
            pl.BlockSpec((tm, D), lambda j, *_: (j, 0)),
            pl.BlockSpec((1, D), lambda j, *_: (0, 0)),
        ] + _table_specs(tm) + [
            pl.BlockSpec(memory_space=pl.ANY),
        ],
        out_specs=pl.BlockSpec((tm, D), lambda j, *_: (j, 0)),
        scratch_shapes=[pltpu.VMEM((2, N_EXPERTS, tm, D), F32), pltpu.SemaphoreType.DMA((2,))],
    )
    return pl.pallas_call(
        _finish_kernel,
        grid_spec=grid_spec,
        out_shape=jax.ShapeDtypeStruct((T, D), F32),
        compiler_params=_params(("arbitrary",)),
        name="moe_finish",
    )(offs, short, route, h, g, *tables, ys)


def _route_metadata(route, tm):
    T = route.shape[0]
    nt = T // tm
    sel = (route[:, N_EXPERTS:2 * N_EXPERTS] > 0.5).astype(jnp.int32)
    cnt_tile = sel.reshape(nt, tm, N_EXPERTS).sum(axis=1)
    short = (jnp.max(cnt_tile, axis=1) <= SHORT_BLOCK).astype(jnp.int32)
    cnt_tile = ((cnt_tile + SUBLANES - 1) // SUBLANES) * SUBLANES
    n_e = cnt_tile.sum(axis=0)
    padded = ((n_e + tm - 1) // tm + 1) * tm
    ends = jnp.cumsum(padded)
    starts = ends - padded
    offs = starts[None, :] + jnp.cumsum(cnt_tile, axis=0) - cnt_tile
    fill = starts + n_e
    max_rows = TOP_K * T + nt * N_EXPERTS * (SUBLANES - 1)
    G = -(-max_rows // tm) + 2 * N_EXPERTS + 1
    tile_start = jnp.arange(G, dtype=jnp.int32) * tm
    tile_expert = jnp.minimum(jnp.sum((tile_start[:, None] >= ends[None, :]).astype(jnp.int32), axis=1),
                              N_EXPERTS - 1)
    active = (tile_start < fill[tile_expert]).astype(jnp.int32)
    x_block = jnp.where(active > 0, jnp.arange(G, dtype=jnp.int32), jnp.argmax(active).astype(jnp.int32))
    fill = jnp.concatenate([fill, ends[-1:] // tm])
    return (offs.reshape(-1).astype(jnp.int32), short, fill.astype(jnp.int32), tile_expert.astype(jnp.int32),
            x_block, active, G * tm)


def kernel(x, attn_norm, ffn_norm, hgrn_w_in, hgrn_lower_bounds, hgrn_out_norm, hgrn_w_out, kv_norm, w_kv,
           dil_w_q, dil_w_out, ffn_w_gate, ffn_w_up, ffn_w_down, moe_w_router, moe_w_gate, moe_w_up,
           moe_w_down, final_norm):
    B, S, D = x.shape
    row = lambda v: v.reshape(1, -1).astype(F32)

    lbs = jnp.cumsum(jax.nn.softmax(hgrn_lower_bounds.astype(F32), axis=0), axis=0)
    h = _hgrn_mixer(x, row(attn_norm[0]), hgrn_w_in[0].astype(BF16), row(lbs[0]), row(hgrn_out_norm[0]),
                    hgrn_w_out[0].astype(BF16))
    h = _dense_ffn(h, row(ffn_norm[0]), ffn_w_gate[0].astype(BF16), ffn_w_up[0].astype(BF16),
                   ffn_w_down[0].astype(BF16))

    cos_k, sin_k = _rope_tables(S, 1.0)
    cos_q, sin_q = _rope_tables(S, ATT_SCALE * LOG2E)
    w_k = _pair_block_columns(w_kv[:, :N_BRANCH * D])
    w_v = w_kv[:, N_BRANCH * D:].astype(BF16)
    w_q = _pair_block_columns(dil_w_q[0])
    ks = _norm_proj(h, row(kv_norm), w_k, cos_k, sin_k, rope=True)
    vs = _norm_proj(h, row(kv_norm), w_v, cos_k, sin_k, rope=False)
    qs = _norm_proj(h, row(attn_norm[1]), w_q, cos_q, sin_q, rope=True)
    os, lses = [], []
    for i, dil in enumerate(DILATIONS):
        o_i, lse_i = _dilated_attn(qs[i], ks[i], vs[i], dil)
        os.append(o_i)
        lses.append(lse_i)

    head_of_col = jnp.arange(D, dtype=jnp.int32) // ATT_HEAD_DIM
    expand = (jnp.arange(LANES, dtype=jnp.int32)[:, None] == head_of_col[None, :]).astype(BF16)
    expand = jnp.concatenate([expand, expand], axis=0)
    wr = jnp.zeros((D, LANES), F32).at[:, :N_EXPERTS].set(moe_w_router[0].astype(F32))
    wr_hi = wr.astype(BF16)
    wr_lo = (wr - wr_hi.astype(F32)).astype(BF16)
    h, hn, route = _attn_combine(os, lses, h, dil_w_out[0].astype(BF16), expand, row(ffn_norm[1]),
                                 jnp.concatenate([wr_hi, wr_lo], axis=1))

    offs, short, fill, tile_expert, x_block, active, n_rows = _route_metadata(route, TOKEN_TILE)
    tables = _spread_tables(SHORT_BLOCK) + _spread_tables(TOKEN_TILE)
    xs = _moe_dispatch(route, hn, offs, short, fill, tables, n_rows)
    ys = _moe_experts(xs, tile_expert, x_block, active, moe_w_gate[0].astype(BF16), moe_w_up[0].astype(BF16),
                      moe_w_down[0].astype(BF16))
    out = _moe_finish(ys, offs, short, route, h, row(final_norm), tables)
    return out.reshape(B, S, D)
```

```python
import functools

import jax
import jax.numpy as jnp
from jax import lax
from jax.experimental import pallas as pl
from jax.experimental.pallas import tpu as pltpu

F32 = jnp.float32
BF16 = jnp.bfloat16

D_MODEL = 1024
HGRN_HEAD_DIM = 128
HGRN_HEADS = D_MODEL // HGRN_HEAD_DIM
HGRN_CHUNK = 64
ATT_HEAD_DIM = 64
ATT_HEADS = D_MODEL // ATT_HEAD_DIM
DILATIONS = (1, 4, 16)
N_BRANCH = len(DILATIONS)
BAND = 128
ATT_SCALE = ATT_HEAD_DIM ** -0.5
ROPE_THETA = 10000.0
N_EXPERTS = 8
TOP_K = 2
EPS = 1e-6

LANES = 128
SUBLANES = 8
TOKEN_TILE = 256
WIDE_TILE = 512
MOE_ROW_TILE = TOKEN_TILE
VMEM_LIMIT = 56 * 1024 * 1024
NEG_BIG = -1e30
LN2 = 0.6931471805599453
LOG2E = 1.0 / LN2


def _dot(a, b):
    return jnp.dot(a, b, preferred_element_type=F32)


def _dot_nt(a, b):
    return lax.dot_general(a, b, (((1,), (1,)), ((), ())), preferred_element_type=F32)


def _dot_tn(a, b):
    return lax.dot_general(a, b, (((0,), (0,)), ((), ())), preferred_element_type=F32)


def _rms(x, g):
    return x * lax.rsqrt(jnp.mean(x * x, axis=-1, keepdims=True) + EPS) * g


def _sigmoid(x):
    return 1.0 / (1.0 + jnp.exp(-x))


def _silu(x):
    return x * _sigmoid(x)


def _split3(x):
    hi = x.astype(BF16)
    r1 = x - hi.astype(F32)
    mid = r1.astype(BF16)
    lo = (r1 - mid.astype(F32)).astype(BF16)
    return hi, mid, lo


def _const_spec(shape):
    nd = len(shape)
    return pl.BlockSpec(shape, lambda *_: (0,) * nd, pipeline_mode=pl.Buffered(1))


def _params(sem):
    return pltpu.CompilerParams(dimension_semantics=sem, vmem_limit_bytes=VMEM_LIMIT)


def _hgrn_kernel(x_ref, g_ref, win_ref, lb_ref, onorm_ref, wout_ref, out_ref,
                 state_ref, y_ref, o_ref, k_ref, b_ref, oi_ref, s_ref):
    tm = x_ref.shape[1]
    dk = HGRN_HEAD_DIM
    c_len = HGRN_CHUNK

    @pl.when(pl.program_id(1) == 0)
    def _():
        state_ref[...] = jnp.zeros_like(state_ref)

    x = x_ref[0]
    xn = _rms(x, g_ref[...]).astype(BF16)
    y_ref[...] = _dot(xn, win_ref[...])

    r = lax.broadcasted_iota(jnp.int32, (tm, tm), 0)
    c = lax.broadcasted_iota(jnp.int32, (tm, tm), 1)
    tri = jnp.where(c <= r, 1.0, 0.0).astype(BF16)

    lb = lb_ref[...]
    f = lb + (1.0 - lb) * _sigmoid(y_ref[:, D_MODEL:2 * D_MODEL])
    k_ref[...] = 1.0 - f
    lf_hi, lf_mid, lf_lo = _split3(jnp.log(f))
    b_ref[...] = _dot(tri, lf_hi) + _dot(tri, lf_mid) + _dot(tri, lf_lo)

    for h in range(HGRN_HEADS):
        cols = slice(h * dk, (h + 1) * dk)
        q = y_ref[:, h * dk:(h + 1) * dk]
        v16 = y_ref[:, 2 * D_MODEL + h * dk:2 * D_MODEL + (h + 1) * dk].astype(BF16)
        k = k_ref[:, cols]
        b = b_ref[:, cols]
        b_last = b[tm - 1:tm, :]
        st = state_ref[h]
        oi_ref[:, cols] = _dot_nt((q * jnp.exp(b)).astype(BF16), st.astype(BF16))
        k_end = (k * jnp.exp(b_last - b)).astype(BF16)
        state_ref[h] = st * jnp.exp(b_last) + _dot_tn(v16, k_end)
        for ci in range(tm // c_len):
            rows = slice(ci * c_len, (ci + 1) * c_len)
            n_keys = (ci + 1) * c_len
            ref = b[ci * c_len - 1:ci * c_len, :] if ci > 0 else jnp.zeros((1, dk), F32)
            q_dec = (q[rows] * jnp.exp(b[rows] - ref)).astype(BF16)
            k_dec = (k[:n_keys] * jnp.exp(ref - b[:n_keys])).astype(BF16)
            qpos = lax.broadcasted_iota(jnp.int32, (c_len, n_keys), 0) + ci * c_len
            kpos = lax.broadcasted_iota(jnp.int32, (c_len, n_keys), 1)
            s_ref[h, rows, :n_keys] = jnp.where(kpos <= qpos, _dot_nt(q_dec, k_dec), 0.0).astype(BF16)
            if n_keys < tm:
                s_ref[h, rows, n_keys:] = jnp.zeros((c_len, tm - n_keys), BF16)

    for h in range(HGRN_HEADS):
        cols = slice(h * dk, (h + 1) * dk)
        v16 = y_ref[:, 2 * D_MODEL + h * dk:2 * D_MODEL + (h + 1) * dk].astype(BF16)
        gate = y_ref[:, 3 * D_MODEL + h * dk:3 * D_MODEL + (h + 1) * dk]
        o = _dot(s_ref[h], v16) + oi_ref[:, cols]
        o_ref[:, cols] = (_rms(o, onorm_ref[...]) * _silu(gate)).astype(BF16)

    out_ref[0] = x + _dot(o_ref[...], wout_ref[...])


def _hgrn_mixer(x, g, w_in, lb, out_norm, w_out):
    B, S, D = x.shape
    tm = TOKEN_TILE
    return pl.pallas_call(
        _hgrn_kernel,
        grid=(B, S // tm),
        in_specs=[
            pl.BlockSpec((1, tm, D), lambda b, s: (b, s, 0)),
            _const_spec((1, D)),
            _const_spec((D, 4 * D)),
            _const_spec((1, D)),
            _const_spec((1, HGRN_HEAD_DIM)),
            _const_spec((D, D)),
        ],
        out_specs=pl.BlockSpec((1, tm, D), lambda b, s: (b, s, 0)),
        out_shape=jax.ShapeDtypeStruct((B, S, D), F32),
        scratch_shapes=[
            pltpu.VMEM((HGRN_HEADS, HGRN_HEAD_DIM, HGRN_HEAD_DIM), F32),
            pltpu.VMEM((tm, 4 * D), F32),
            pltpu.VMEM((tm, D), BF16),
            pltpu.VMEM((tm, D), F32),
            pltpu.VMEM((tm, D), F32),
            pltpu.VMEM((tm, D), F32),
            pltpu.VMEM((HGRN_HEADS, tm, tm), BF16),
        ],
        compiler_params=_params(("arbitrary", "arbitrary")),
        name="hgrn_mixer",
    )(x, g, w_in, lb, out_norm, w_out)


def _ffn_kernel(x_ref, g_ref, wg_ref, wu_ref, wd_ref, out_ref):
    x = x_ref[0]
    xn = _rms(x, g_ref[...]).astype(BF16)
    a = _silu(_dot(xn, wg_ref[...])) * _dot(xn, wu_ref[...])
    out_ref[0] = x + _dot(a.astype(BF16), wd_ref[...])


def _dense_ffn(x, g, w_gate, w_up, w_down):
    B, S, D = x.shape
    Fd = w_gate.shape[1]
    tm = WIDE_TILE
    return pl.pallas_call(
        _ffn_kernel,
        grid=(B, S // tm),
        in_specs=[
            pl.BlockSpec((1, tm, D), lambda b, s: (b, s, 0)),
            _const_spec((1, D)),
            _const_spec((D, Fd)),
            _const_spec((D, Fd)),
            _const_spec((Fd, D)),
        ],
        out_specs=pl.BlockSpec((1, tm, D), lambda b, s: (b, s, 0)),
        out_shape=jax.ShapeDtypeStruct((B, S, D), F32),
        compiler_params=_params(("arbitrary", "arbitrary")),
        name="dense_ffn",
    )(x, g, w_gate, w_up, w_down)


def _proj_kernel(x_ref, g_ref, w_ref, cos_ref, sin_ref, o0_ref, o1_ref, o2_ref, t_ref, *, rope):
    tm = x_ref.shape[1]
    xn = _rms(x_ref[0], g_ref[...]).astype(BF16)
    out_refs = (o0_ref, o1_ref, o2_ref)
    for j, dil in enumerate(DILATIONS):
        t = _dot(xn, w_ref[:, j * D_MODEL:(j + 1) * D_MODEL])
        for p in range(D_MODEL // LANES):
            cols = slice(p * LANES, (p + 1) * LANES)
            tp = t[:, cols]
            if rope:
                tp = tp * cos_ref[...] + pltpu.roll(tp, LANES // 2, 1) * sin_ref[...]
            if dil == 1:
                out_refs[j][0, 0, :, cols] = tp.astype(BF16)
            else:
                t_ref[p] = tp
                for r in range(dil):
                    out_refs[j][0, r, :, cols] = t_ref[p, pl.ds(r, tm // dil, stride=dil), :].astype(BF16)


def _norm_proj(x, g, w, cos, sin, rope):
    B, S, D = x.shape
    tm = WIDE_TILE
    return pl.pallas_call(
        functools.partial(_proj_kernel, rope=rope),
        grid=(B, S // tm),
        in_specs=[
            pl.BlockSpec((1, tm, D), lambda b, s: (b, s, 0)),
            _const_spec((1, D)),
            _const_spec((D, N_BRANCH * D)),
            pl.BlockSpec((tm, LANES), lambda b, s: (s, 0)),
            pl.BlockSpec((tm, LANES), lambda b, s: (s, 0)),
        ],
        out_specs=[pl.BlockSpec((1, dil, tm // dil, D), lambda b, s: (b, 0, s, 0)) for dil in DILATIONS],
        out_shape=[jax.ShapeDtypeStruct((B, dil, S // dil, D), BF16) for dil in DILATIONS],
        scratch_shapes=[pltpu.VMEM((D // LANES, tm, LANES), F32)],
        compiler_params=_params(("arbitrary", "arbitrary")),
        name="norm_proj_rope" if rope else "norm_proj",
    )(x, g, w, cos, sin)


def _rope_tables(seq_len, scale):
    half = ATT_HEAD_DIM // 2
    inv_freq = ROPE_THETA ** (-jnp.arange(half, dtype=F32) / half)
    ang = jnp.arange(seq_len, dtype=F32)[:, None] * inv_freq[None, :]
    cos = jnp.tile(jnp.cos(ang), (1, LANES // half))
    sin = jnp.tile(jnp.sin(ang), (1, LANES // half))
    sign = jnp.where(jnp.arange(LANES) < LANES // 2, -1.0, 1.0).astype(F32)
    return cos * scale, sin * sign[None, :] * scale


def _pair_block_columns(w):
    rows = w.shape[0]
    half = ATT_HEAD_DIM // 2
    w = w.reshape(rows, -1, 2, 2, half)
    return jnp.swapaxes(w, 2, 3).reshape(rows, -1).astype(BF16)


def _attn_kernel(q_ref, kp_ref, kc_ref, vp_ref, vc_ref, o_ref, lse_ref):
    n = pl.program_id(2)
    qi = lax.broadcasted_iota(jnp.int32, (BAND, 2 * BAND), 0)
    ki = lax.broadcasted_iota(jnp.int32, (BAND, 2 * BAND), 1)
    dist = qi + BAND - ki
    first_key = jnp.where(n > 0, 0, BAND)
    valid = (dist >= 0) & (dist <= BAND) & (ki >= first_key)
    lane = lax.broadcasted_iota(jnp.int32, (BAND, LANES), 1)
    m_tile = jnp.zeros((BAND, LANES), F32)
    l_tile = jnp.ones((BAND, LANES), F32)
    for p in range(ATT_HEADS // 2):
        cols = slice(p * LANES, (p + 1) * LANES)
        qp = q_ref[:, cols]
        kcat = jnp.concatenate([kp_ref[:, cols], kc_ref[:, cols]], axis=0)
        vcat = jnp.concatenate([vp_ref[:, cols], vc_ref[:, cols]], axis=0)
        o_pair = jnp.zeros((BAND, LANES), F32)
        for a in range(2):
            qm = jnp.where((lane // (ATT_HEAD_DIM // 2)) % 2 == a, qp, jnp.zeros_like(qp))
            s = jnp.where(valid, _dot_nt(qm, kcat), NEG_BIG)
            m = jnp.max(s, axis=-1, keepdims=True)
            pr = jnp.exp2(s - m)
            l = jnp.sum(pr, axis=-1, keepdims=True)
            o = _dot(pr.astype(BF16), vcat) / l
            o_pair = jnp.where(lane // ATT_HEAD_DIM == a, o, o_pair)
            m_tile = jnp.where(lane == 2 * p + a, m, m_tile)
            l_tile = jnp.where(lane == 2 * p + a, l, l_tile)
        o_ref[:, cols] = o_pair.astype(o_ref.dtype)
    lse_ref[...] = m_tile * LN2 + jnp.log(l_tile)


def _dilated_attn(q, k, v, dil):
    B, _, L, D = q.shape
    nb = L // BAND
    blk = (None, None, BAND, D)
    cur = lambda b, r, n: (b, r, n, 0)
    prev = lambda b, r, n: (b, r, jnp.maximum(n - 1, 0), 0)
    return pl.pallas_call(
        _attn_kernel,
        grid=(B, dil, nb),
        in_specs=[
            pl.BlockSpec(blk, cur),
            pl.BlockSpec(blk, prev),
            pl.BlockSpec(blk, cur),
            pl.BlockSpec(blk, prev),
            pl.BlockSpec(blk, cur),
        ],
        out_specs=[pl.BlockSpec(blk, cur), pl.BlockSpec((None, None, BAND, LANES), cur)],
        out_shape=[
            jax.ShapeDtypeStruct((B, dil, L, D), BF16),
            jax.ShapeDtypeStruct((B, dil, L, LANES), F32),
        ],
        compiler_params=_params(("arbitrary", "arbitrary", "arbitrary")),
        name=f"dilated_attn_d{dil}",
    )(q, k, k, v, v)


def _combine_kernel(o0_ref, o1_ref, o2_ref, l0_ref, l1_ref, l2_ref, h_ref, wout_ref, expand_ref,
                    g_ref, wr_ref, h_out_ref, hn_out_ref, route_ref, o_scr, l_scr):
    tm = h_ref.shape[1]
    o_refs = (o0_ref, o1_ref, o2_ref)
    l_refs = (l0_ref, l1_ref, l2_ref)
    for i, dil in enumerate(DILATIONS):
        for r in range(dil):
            rows = pl.ds(r, tm // dil, stride=dil)
            l_scr[i, rows, :] = l_refs[i][0, r]
            for p in range(D_MODEL // LANES):
                o_scr[i, p, rows, :] = o_refs[i][0, r, :, p * LANES:(p + 1) * LANES].astype(F32)
    lses = [l_scr[i] for i in range(N_BRANCH)]
    m = jnp.maximum(jnp.maximum(lses[0], lses[1]), lses[2])
    es = [jnp.exp(l - m) for l in lses]
    den = es[0] + es[1] + es[2]
    o = None
    for i in range(N_BRANCH):
        w = es[i] / den
        w_hi = w.astype(BF16)
        w_lo = (w - w_hi.astype(F32)).astype(BF16)
        w_full = _dot(jnp.concatenate([w_hi, w_lo], axis=1), expand_ref[...])
        term = w_full * jnp.concatenate([o_scr[i, p] for p in range(D_MODEL // LANES)], axis=1)
        o = term if o is None else o + term
    h = h_ref[0] + _dot(o.astype(BF16), wout_ref[...])
    h_out_ref[...] = h
    hn = _rms(h, g_ref[...])
    hn_hi = hn.astype(BF16)
    hn_out_ref[...] = hn_hi

    hn_lo = (hn - hn_hi.astype(F32)).astype(BF16)
    l_hi = _dot(hn_hi, wr_ref[...])
    logits = l_hi[:, :LANES] + l_hi[:, LANES:] + _dot(hn_lo, wr_ref[:, :LANES])
    lane = lax.broadcasted_iota(jnp.int32, logits.shape, 1)
    lane_f = lane.astype(F32)
    logits = jnp.where(lane < N_EXPERTS, logits, NEG_BIG)
    mx = jnp.max(logits, axis=-1, keepdims=True)
    pe = jnp.exp(logits - mx)
    probs = pe / jnp.sum(pe, axis=-1, keepdims=True)
    probs = jnp.where(lane < N_EXPERTS, probs, -1.0)
    p1 = jnp.max(probs, axis=-1, keepdims=True)
    i1 = jnp.min(jnp.where(probs == p1, lane_f, float(LANES)), axis=-1, keepdims=True)
    rest = jnp.where(lane_f == i1, -1.0, probs)
    p2 = jnp.max(rest, axis=-1, keepdims=True)
    i2 = jnp.min(jnp.where(rest == p2, lane_f, float(LANES)), axis=-1, keepdims=True)
    tot = p1 + p2
    route = jnp.where(lane_f == i1, p1 / tot, 0.0)
    route = jnp.where(lane_f == i2, p2 / tot, route)
    sel = jnp.where((lane_f == i1 + N_EXPERTS) | (lane_f == i2 + N_EXPERTS), 1.0, 0.0)
    route_ref[...] = route + sel


def _attn_combine(os, lses, h, w_out, expand, g, wr):
    B, S, D = h.shape
    T = B * S
    tm = WIDE_TILE
    ts = S // tm
    row = lambda b, s: (b * ts + s, 0)
    o_specs = [pl.BlockSpec((1, dil, tm // dil, D), lambda b, s: (b, 0, s, 0)) for dil in DILATIONS]
    l_specs = [pl.BlockSpec((1, dil, tm // dil, LANES), lambda b, s: (b, 0, s, 0)) for dil in DILATIONS]
    return pl.pallas_call(
        _combine_kernel,
        grid=(B, ts),
        in_specs=o_specs + l_specs + [
            pl.BlockSpec((1, tm, D), lambda b, s: (b, s, 0)),
            _const_spec((D, D)),
            _const_spec((2 * LANES, D)),
            _const_spec((1, D)),
            _const_spec((D, 2 * LANES)),
        ],
        out_specs=[pl.BlockSpec((tm, D), row), pl.BlockSpec((tm, D), row), pl.BlockSpec((tm, LANES), row)],
        out_shape=[
            jax.ShapeDtypeStruct((T, D), F32),
            jax.ShapeDtypeStruct((T, D), BF16),
            jax.ShapeDtypeStruct((T, LANES), F32),
        ],
        scratch_shapes=[pltpu.VMEM((N_BRANCH, D // LANES, tm, LANES), F32),
                        pltpu.VMEM((N_BRANCH, tm, LANES), F32)],
        compiler_params=_params(("arbitrary", "arbitrary")),
        name="attn_combine_route",
    )(*os, *lses, h, w_out, expand, g, wr)


SHORT_BLOCK = 96


def _spread_tables(n_slots):
    n_cols = N_EXPERTS * n_slots
    col_expert = jnp.arange(n_cols, dtype=jnp.int32) // n_slots
    src = jnp.arange(LANES, dtype=jnp.int32)
    spread = (src[:, None] == col_expert[None, :] + N_EXPERTS).astype(BF16)
    slots = (jnp.arange(n_cols, dtype=jnp.int32) % n_slots).astype(F32).reshape(1, n_cols)
    return jnp.concatenate([spread, spread], axis=0), slots


def _slot_one_hot(route, spread_ref, slots_ref, weighted=False):
    tm = route.shape[0]
    lane = lax.broadcasted_iota(jnp.int32, route.shape, 1)
    sel = jnp.where((lane >= N_EXPERTS) & (lane < 2 * N_EXPERTS), route, 0.0)
    r = lax.broadcasted_iota(jnp.int32, (tm, tm), 0)
    c = lax.broadcasted_iota(jnp.int32, (tm, tm), 1)
    strict_lower = jnp.where(c < r, 1.0, 0.0).astype(BF16)
    rank = _dot(strict_lower, sel.astype(BF16))
    key = jnp.where(sel > 0.5, rank, -1.0).astype(BF16)
    hit = _dot(key, spread_ref[:LANES, :]) == slots_ref[...]
    if not weighted:
        return jnp.where(hit, 1.0, 0.0)
    w = pltpu.roll(route, N_EXPERTS, 1)
    w_hi = w.astype(BF16)
    w_lo = (w - w_hi.astype(F32)).astype(BF16)
    return jnp.where(hit, _dot(jnp.concatenate([w_hi, w_lo], axis=1), spread_ref[...]), 0.0)


def _block_copy(src, dst, sem):
    return pltpu.make_async_copy(src, dst, sem)


def _when_short(short_ref, step, fn, tables, tm):
    @pl.when(short_ref[step] > 0)
    def _():
        fn(SHORT_BLOCK, *tables[:2])

    @pl.when(short_ref[step] == 0)
    def _():
        fn(tm, *tables[2:])


def _dispatch_kernel(offs_ref, short_ref, fill_ref, route_ref, hn_ref, sp_s, sl_s, sp_l, sl_l, xs_hbm, stage, sem):
    tables = (sp_s, sl_s, sp_l, sl_l)
    j = pl.program_id(0)
    n = pl.num_programs(0)
    tm = hn_ref.shape[0]
    n_tiles = xs_hbm.shape[0] // tm
    slot = j % 2

    def zero_copy(row):
        return _block_copy(stage.at[1, 0], xs_hbm.at[pl.ds(pl.multiple_of(row, SUBLANES), tm)], sem.at[1])

    @pl.when(j == 0)
    def _():
        stage[1, 0] = jnp.zeros((tm, D_MODEL), F32)
        first_tail = fill_ref[N_EXPERTS]

        def start_tail(t, carry):
            zero_copy(t * tm).start()
            return carry

        def wait_tail(t, carry):
            zero_copy(t * tm).wait()
            return carry

        first = [zero_copy(fill_ref[e]) for e in range(N_EXPERTS)]
        second = [zero_copy(fill_ref[e] + tm) for e in range(N_EXPERTS)]
        for cp in first:
            cp.start()
        lax.fori_loop(first_tail, n_tiles, start_tail, 0)
        for cp in first:
            cp.wait()
        lax.fori_loop(first_tail, n_tiles, wait_tail, 0)
        for cp in second:
            cp.start()
        for cp in second:
            cp.wait()

    def copies(step_slot, step, n_rows):
        return [_block_copy(stage.at[step_slot, e, pl.ds(0, n_rows)],
                            xs_hbm.at[pl.ds(pl.multiple_of(offs_ref[step * N_EXPERTS + e], SUBLANES), n_rows)],
                            sem.at[step_slot])
                for e in range(N_EXPERTS)]

    def place(n_rows, spread_ref, slots_ref):
        one_hot = _slot_one_hot(route_ref[...], spread_ref, slots_ref).astype(BF16)
        placed = _dot_tn(one_hot, hn_ref[...])
        for e in range(N_EXPERTS):
            stage[slot, e, :n_rows] = placed[e * n_rows:(e + 1) * n_rows]

    _when_short(short_ref, j, place, tables, tm)

    @pl.when(j > 0)
    def _():
        _when_short(short_ref, j - 1, lambda n_rows, *_: [cp.wait() for cp in copies(1 - slot, j - 1, n_rows)],
                    tables, tm)

    _when_short(short_ref, j, lambda n_rows, *_: [cp.start() for cp in copies(slot, j, n_rows)], tables, tm)

    @pl.when(j == n - 1)
    def _():
        _when_short(short_ref, j, lambda n_rows, *_: [cp.wait() for cp in copies(slot, j, n_rows)], tables, tm)


def _table_specs(tm):
    return [_const_spec((2 * LANES, N_EXPERTS * SHORT_BLOCK)), _const_spec((1, N_EXPERTS * SHORT_BLOCK)),
            _const_spec((2 * LANES, N_EXPERTS * tm)), _const_spec((1, N_EXPERTS * tm))]


def _moe_dispatch(route, hn, offs, short, fill, tables, n_rows):
    T, D = hn.shape
    tm = TOKEN_TILE
    grid_spec = pltpu.PrefetchScalarGridSpec(
        num_scalar_prefetch=3,
        grid=(T // tm,),
        in_specs=[
            pl.BlockSpec((tm, LANES), lambda j, *_: (j, 0)),
            pl.BlockSpec((tm, D), lambda j, *_: (j, 0)),
        ] + _table_specs(tm),
        out_specs=pl.BlockSpec(memory_space=pl.ANY),
        scratch_shapes=[pltpu.VMEM((2, N_EXPERTS, tm, D), F32), pltpu.SemaphoreType.DMA((2,))],
    )
    return pl.pallas_call(
        _dispatch_kernel,
        grid_spec=grid_spec,
        out_shape=jax.ShapeDtypeStruct((n_rows, D), F32),
        compiler_params=_params(("arbitrary",)),
        name="moe_dispatch",
    )(offs, short, fill, route, hn, *tables)


EXPERT_F_CHUNK = 512


def _expert_kernel(te_ref, xb_ref, act_ref, first_ref, x_ref, wg_hbm, wu_hbm, wd_hbm, y_ref,
                   wg_buf, wu_buf, wd_buf, stg_gu, stg_d, sem):
    del xb_ref
    i = pl.program_id(0)
    tm = x_ref.shape[0]
    fc = EXPERT_F_CHUNK
    n_chunks = wg_buf.shape[1] // fc

    def chunk_copies(e, c, slot):
        cols = pl.ds(c * fc, fc)
        return [pltpu.make_async_copy(wg_hbm.at[e, :, cols], stg_gu.at[slot, 0], sem.at[slot]),
                pltpu.make_async_copy(wu_hbm.at[e, :, cols], stg_gu.at[slot, 1], sem.at[slot]),
                pltpu.make_async_copy(wd_hbm.at[e, cols, :], stg_d.at[slot], sem.at[slot])]

    def convert_chunk(c, slot):
        cs = slice(c * fc, (c + 1) * fc)
        wg_buf[:, cs] = stg_gu[slot, 0].astype(BF16)
        wu_buf[:, cs] = stg_gu[slot, 1].astype(BF16)
        wd_buf[cs, :] = stg_d[slot].astype(BF16)

    def load_expert(e):
        for cp in chunk_copies(e, 0, 0):
            cp.start()
        for c in range(n_chunks):
            if c + 1 < n_chunks:
                for cp in chunk_copies(e, c + 1, (c + 1) % 2):
                    cp.start()
            for cp in chunk_copies(e, c, c % 2):
                cp.wait()
            convert_chunk(c, c % 2)

    @pl.when(first_ref[i] > 0)
    def _():
        load_expert(te_ref[i])

    @pl.when(act_ref[i] > 0)
    def _():
        x = x_ref[...].astype(BF16)
        acc = jnp.zeros((tm, D_MODEL), F32)
        for c in range(n_chunks):
            cs = slice(c * fc, (c + 1) * fc)
            a = _silu(_dot(x, wg_buf[:, cs])) * _dot(x, wu_buf[:, cs])
            acc = acc + _dot(a.astype(BF16), wd_buf[cs, :])
        y_ref[...] = acc

    @pl.when(act_ref[i] == 0)
    def _():
        y_ref[...] = jnp.zeros_like(y_ref)


def _moe_experts(xs, tile_expert, x_block, active, first, w_gate, w_up, w_down):
    n_rows, D = xs.shape
    Fd = w_gate.shape[2]
    tm = MOE_ROW_TILE
    fc = EXPERT_F_CHUNK
    G = n_rows // tm
    grid_spec = pltpu.PrefetchScalarGridSpec(
        num_scalar_prefetch=4,
        grid=(G,),
        in_specs=[
            pl.BlockSpec((tm, D), lambda i, te, xb, *_: (xb[i], 0)),
            pl.BlockSpec(memory_space=pl.ANY),
            pl.BlockSpec(memory_space=pl.ANY),
            pl.BlockSpec(memory_space=pl.ANY),
        ],
        out_specs=pl.BlockSpec((tm, D), lambda i, *_: (i, 0)),
        scratch_shapes=[
            pltpu.VMEM((D, Fd), BF16),
            pltpu.VMEM((D, Fd), BF16),
            pltpu.VMEM((Fd, D), BF16),
            pltpu.VMEM((2, 2, D, fc), F32),
            pltpu.VMEM((2, fc, D), F32),
            pltpu.SemaphoreType.DMA((2,)),
        ],
    )
    return pl.pallas_call(
        _expert_kernel,
        grid_spec=grid_spec,
        out_shape=jax.ShapeDtypeStruct((n_rows, D), F32),
        compiler_params=_params(("arbitrary",)),
        name="moe_experts",
    )(tile_expert, x_block, active, first, xs, w_gate, w_up, w_down)


def _finish_kernel(offs_ref, short_ref, route_ref, h_ref, g_ref, sp_s, sl_s, sp_l, sl_l, ys_hbm, out_ref, ybuf, sem):
    tables = (sp_s, sl_s, sp_l, sl_l)
    j = pl.program_id(0)
    n = pl.num_programs(0)
    tm = h_ref.shape[0]
    slot = j % 2

    def copies(step_slot, step, n_rows):
        return [_block_copy(ys_hbm.at[pl.ds(pl.multiple_of(offs_ref[step * N_EXPERTS + e], SUBLANES), n_rows)],
                            ybuf.at[step_slot, e, pl.ds(0, n_rows)], sem.at[step_slot])
                for e in range(N_EXPERTS)]

    @pl.when(j == 0)
    def _():
        _when_short(short_ref, 0, lambda n_rows, *_: [cp.start() for cp in copies(0, 0, n_rows)], tables, tm)

    @pl.when(j + 1 < n)
    def _():
        _when_short(short_ref, j + 1, lambda n_rows, *_: [cp.start() for cp in copies(1 - slot, j + 1, n_rows)],
                    tables, tm)

    def gather(n_rows, spread_ref, slots_ref):
        w = _slot_one_hot(route_ref[...], spread_ref, slots_ref, weighted=True)
        w_hi = w.astype(BF16)
        w_lo = (w - w_hi.astype(F32)).astype(BF16)
        for cp in copies(slot, j, n_rows):
            cp.wait()
        y = jnp.concatenate([ybuf[slot, e, :n_rows].astype(BF16) for e in range(N_EXPERTS)], axis=0)
        out_ref[...] = _rms(h_ref[...] + _dot(w_hi, y) + _dot(w_lo, y), g_ref[...])

    _when_short(short_ref, j, gather, tables, tm)


def _moe_finish(ys, offs, short, route, h, g, tables):
    T, D = h.shape
    tm = TOKEN_TILE
    grid_spec = pltpu.PrefetchScalarGridSpec(
        num_scalar_prefetch=2,
        grid=(T // tm,),
        in_specs=[
            pl.BlockSpec((tm, LANES), lambda j, *_: (j, 0)),
            pl.BlockSpec((tm, D), lambda j, *_: (j, 0)),
            pl.BlockSpec((1, D), lambda j, *_: (0, 0)),
        ] + _table_specs(tm) + [
            pl.BlockSpec(memory_space=pl.ANY),
        ],
        out_specs=pl.BlockSpec((tm, D), lambda j, *_: (j, 0)),
        scratch_shapes=[pltpu.VMEM((2, N_EXPERTS, tm, D), F32), pltpu.SemaphoreType.DMA((2,))],
    )
    return pl.pallas_call(
        _finish_kernel,
        grid_spec=grid_spec,
        out_shape=jax.ShapeDtypeStruct((T, D), F32),
        compiler_params=_params(("arbitrary",)),
        name="moe_finish",
    )(offs, short, route, h, g, *tables, ys)


def _route_metadata(route, tm):
    T = route.shape[0]
    nt = T // tm
    sel = (route[:, N_EXPERTS:2 * N_EXPERTS] > 0.5).astype(jnp.int32)
    cnt_tile = sel.reshape(nt, tm, N_EXPERTS).sum(axis=1)
    short = (jnp.max(cnt_tile, axis=1) <= SHORT_BLOCK).astype(jnp.int32)
    cnt_tile = ((cnt_tile + SUBLANES - 1) // SUBLANES) * SUBLANES
    n_e = cnt_tile.sum(axis=0)
    padded = ((n_e + tm - 1) // tm + 1) * tm
    ends = jnp.cumsum(padded)
    starts = ends - padded
    offs = starts[None, :] + jnp.cumsum(cnt_tile, axis=0) - cnt_tile
    fill = starts + n_e
    max_rows = TOP_K * T + nt * N_EXPERTS * (SUBLANES - 1)
    G = -(-max_rows // tm) + 2 * N_EXPERTS + 1
    tile_start = jnp.arange(G, dtype=jnp.int32) * tm
    tile_expert = jnp.minimum(jnp.sum((tile_start[:, None] >= ends[None, :]).astype(jnp.int32), axis=1),
                              N_EXPERTS - 1)
    active = (tile_start < fill[tile_expert]).astype(jnp.int32)
    x_block = jnp.where(active > 0, jnp.arange(G, dtype=jnp.int32), jnp.argmax(active).astype(jnp.int32))
    first = ((active > 0) & (tile_start == starts[tile_expert])).astype(jnp.int32)
    fill = jnp.concatenate([fill, ends[-1:] // tm])
    return (offs.reshape(-1).astype(jnp.int32), short, fill.astype(jnp.int32), tile_expert.astype(jnp.int32),
            x_block, active, first, G * tm)


def kernel(x, attn_norm, ffn_norm, hgrn_w_in, hgrn_lower_bounds, hgrn_out_norm, hgrn_w_out, kv_norm, w_kv,
           dil_w_q, dil_w_out, ffn_w_gate, ffn_w_up, ffn_w_down, moe_w_router, moe_w_gate, moe_w_up,
           moe_w_down, final_norm):
    B, S, D = x.shape
    row = lambda v: v.reshape(1, -1).astype(F32)

    lbs = jnp.cumsum(jax.nn.softmax(hgrn_lower_bounds.astype(F32), axis=0), axis=0)
    h = _hgrn_mixer(x, row(attn_norm[0]), hgrn_w_in[0].astype(BF16), row(lbs[0]), row(hgrn_out_norm[0]),
                    hgrn_w_out[0].astype(BF16))
    h = _dense_ffn(h, row(ffn_norm[0]), ffn_w_gate[0].astype(BF16), ffn_w_up[0].astype(BF16),
                   ffn_w_down[0].astype(BF16))

    cos_k, sin_k = _rope_tables(S, 1.0)
    cos_q, sin_q = _rope_tables(S, ATT_SCALE * LOG2E)
    w_k = _pair_block_columns(w_kv[:, :N_BRANCH * D])
    w_v = w_kv[:, N_BRANCH * D:].astype(BF16)
    w_q = _pair_block_columns(dil_w_q[0])
    ks = _norm_proj(h, row(kv_norm), w_k, cos_k, sin_k, rope=True)
    vs = _norm_proj(h, row(kv_norm), w_v, cos_k, sin_k, rope=False)
    qs = _norm_proj(h, row(attn_norm[1]), w_q, cos_q, sin_q, rope=True)
    os, lses = [], []
    for i, dil in enumerate(DILATIONS):
        o_i, lse_i = _dilated_attn(qs[i], ks[i], vs[i], dil)
        os.append(o_i)
        lses.append(lse_i)

    head_of_col = jnp.arange(D, dtype=jnp.int32) // ATT_HEAD_DIM
    expand = (jnp.arange(LANES, dtype=jnp.int32)[:, None] == head_of_col[None, :]).astype(BF16)
    expand = jnp.concatenate([expand, expand], axis=0)
    wr = jnp.zeros((D, LANES), F32).at[:, :N_EXPERTS].set(moe_w_router[0].astype(F32))
    wr_hi = wr.astype(BF16)
    wr_lo = (wr - wr_hi.astype(F32)).astype(BF16)
    h, hn, route = _attn_combine(os, lses, h, dil_w_out[0].astype(BF16), expand, row(ffn_norm[1]),
                                 jnp.concatenate([wr_hi, wr_lo], axis=1))

    offs, short, fill, tile_expert, x_block, active, first, n_rows = _route_metadata(route, TOKEN_TILE)
    tables = _spread_tables(SHORT_BLOCK) + _spread_tables(TOKEN_TILE)
    xs = _moe_dispatch(route, hn, offs, short, fill, tables, n_rows)
    ys = _moe_experts(xs, tile_expert, x_block, active, first, moe_w_gate[0], moe_w_up[0], moe_w_down[0])
    out = _moe_finish(ys, offs, short, route, h, row(final_norm), tables)
    return out.reshape(B, S, D)
```

```python
import functools

import jax
import jax.numpy as jnp
from jax import lax
from jax.experimental import pallas as pl
from jax.experimental.pallas import tpu as pltpu

F32 = jnp.float32
BF16 = jnp.bfloat16

D_MODEL = 1024
HGRN_HEAD_DIM = 128
HGRN_HEADS = D_MODEL // HGRN_HEAD_DIM
HGRN_CHUNK = 64
ATT_HEAD_DIM = 64
ATT_HEADS = D_MODEL // ATT_HEAD_DIM
DILATIONS = (1, 4, 16)
N_BRANCH = len(DILATIONS)
BAND = 128
ATT_SCALE = ATT_HEAD_DIM ** -0.5
ROPE_THETA = 10000.0
N_EXPERTS = 8
TOP_K = 2
EPS = 1e-6

LANES = 128
SUBLANES = 8
TOKEN_TILE = 256
WIDE_TILE = 512
MOE_ROW_TILE = TOKEN_TILE
VMEM_LIMIT = 56 * 1024 * 1024
NEG_BIG = -1e30
LN2 = 0.6931471805599453
LOG2E = 1.0 / LN2


def _dot(a, b):
    return jnp.dot(a, b, preferred_element_type=F32)


def _dot_nt(a, b):
    return lax.dot_general(a, b, (((1,), (1,)), ((), ())), preferred_element_type=F32)


def _dot_tn(a, b):
    return lax.dot_general(a, b, (((0,), (0,)), ((), ())), preferred_element_type=F32)


def _rms(x, g):
    return x * lax.rsqrt(jnp.mean(x * x, axis=-1, keepdims=True) + EPS) * g


def _sigmoid(x):
    return 1.0 / (1.0 + jnp.exp(-x))


def _silu(x):
    return x * _sigmoid(x)


def _split3(x):
    hi = x.astype(BF16)
    r1 = x - hi.astype(F32)
    mid = r1.astype(BF16)
    lo = (r1 - mid.astype(F32)).astype(BF16)
    return hi, mid, lo


def _const_spec(shape):
    nd = len(shape)
    return pl.BlockSpec(shape, lambda *_: (0,) * nd, pipeline_mode=pl.Buffered(1))


def _params(sem):
    return pltpu.CompilerParams(dimension_semantics=sem, vmem_limit_bytes=VMEM_LIMIT)


def _hgrn_kernel(x_ref, g_ref, win_ref, lb_ref, onorm_ref, wout_ref, out_ref,
                 state_ref, y_ref, o_ref, k_ref, b_ref, oi_ref, s_ref):
    tm = x_ref.shape[1]
    dk = HGRN_HEAD_DIM
    c_len = HGRN_CHUNK

    @pl.when(pl.program_id(1) == 0)
    def _():
        state_ref[...] = jnp.zeros_like(state_ref)

    x = x_ref[0]
    xn = _rms(x, g_ref[...]).astype(BF16)
    y_ref[...] = _dot(xn, win_ref[...])

    r = lax.broadcasted_iota(jnp.int32, (tm, tm), 0)
    c = lax.broadcasted_iota(jnp.int32, (tm, tm), 1)
    tri = jnp.where(c <= r, 1.0, 0.0).astype(BF16)

    lb = lb_ref[...]
    f = lb + (1.0 - lb) * _sigmoid(y_ref[:, D_MODEL:2 * D_MODEL])
    k_ref[...] = 1.0 - f
    lf_hi, lf_mid, lf_lo = _split3(jnp.log(f))
    b_ref[...] = _dot(tri, lf_hi) + _dot(tri, lf_mid) + _dot(tri, lf_lo)

    for h in range(HGRN_HEADS):
        cols = slice(h * dk, (h + 1) * dk)
        q = y_ref[:, h * dk:(h + 1) * dk]
        v16 = y_ref[:, 2 * D_MODEL + h * dk:2 * D_MODEL + (h + 1) * dk].astype(BF16)
        k = k_ref[:, cols]
        b = b_ref[:, cols]
        b_last = b[tm - 1:tm, :]
        st = state_ref[h]
        oi_ref[:, cols] = _dot_nt((q * jnp.exp(b)).astype(BF16), st.astype(BF16))
        k_end = (k * jnp.exp(b_last - b)).astype(BF16)
        state_ref[h] = st * jnp.exp(b_last) + _dot_tn(v16, k_end)
        for ci in range(tm // c_len):
            rows = slice(ci * c_len, (ci + 1) * c_len)
            n_keys = (ci + 1) * c_len
            ref = b[ci * c_len - 1:ci * c_len, :] if ci > 0 else jnp.zeros((1, dk), F32)
            q_dec = (q[rows] * jnp.exp(b[rows] - ref)).astype(BF16)
            k_dec = (k[:n_keys] * jnp.exp(ref - b[:n_keys])).astype(BF16)
            qpos = lax.broadcasted_iota(jnp.int32, (c_len, n_keys), 0) + ci * c_len
            kpos = lax.broadcasted_iota(jnp.int32, (c_len, n_keys), 1)
            s_ref[h, rows, :n_keys] = jnp.where(kpos <= qpos, _dot_nt(q_dec, k_dec), 0.0).astype(BF16)
            if n_keys < tm:
                s_ref[h, rows, n_keys:] = jnp.zeros((c_len, tm - n_keys), BF16)

    for h in range(HGRN_HEADS):
        cols = slice(h * dk, (h + 1) * dk)
        v16 = y_ref[:, 2 * D_MODEL + h * dk:2 * D_MODEL + (h + 1) * dk].astype(BF16)
        gate = y_ref[:, 3 * D_MODEL + h * dk:3 * D_MODEL + (h + 1) * dk]
        o = _dot(s_ref[h], v16) + oi_ref[:, cols]
        o_ref[:, cols] = (_rms(o, onorm_ref[...]) * _silu(gate)).astype(BF16)

    out_ref[0] = x + _dot(o_ref[...], wout_ref[...])


def _hgrn_mixer(x, g, w_in, lb, out_norm, w_out):
    B, S, D = x.shape
    tm = TOKEN_TILE
    return pl.pallas_call(
        _hgrn_kernel,
        grid=(B, S // tm),
        in_specs=[
            pl.BlockSpec((1, tm, D), lambda b, s: (b, s, 0)),
            _const_spec((1, D)),
            _const_spec((D, 4 * D)),
            _const_spec((1, D)),
            _const_spec((1, HGRN_HEAD_DIM)),
            _const_spec((D, D)),
        ],
        out_specs=pl.BlockSpec((1, tm, D), lambda b, s: (b, s, 0)),
        out_shape=jax.ShapeDtypeStruct((B, S, D), F32),
        scratch_shapes=[
            pltpu.VMEM((HGRN_HEADS, HGRN_HEAD_DIM, HGRN_HEAD_DIM), F32),
            pltpu.VMEM((tm, 4 * D), F32),
            pltpu.VMEM((tm, D), BF16),
            pltpu.VMEM((tm, D), F32),
            pltpu.VMEM((tm, D), F32),
            pltpu.VMEM((tm, D), F32),
            pltpu.VMEM((HGRN_HEADS, tm, tm), BF16),
        ],
        compiler_params=_params(("arbitrary", "arbitrary")),
        name="hgrn_mixer",
    )(x, g, w_in, lb, out_norm, w_out)


def _ffn_kernel(x_ref, g_ref, wg_ref, wu_ref, wd_ref, out_ref):
    x = x_ref[0]
    xn = _rms(x, g_ref[...]).astype(BF16)
    a = _silu(_dot(xn, wg_ref[...])) * _dot(xn, wu_ref[...])
    out_ref[0] = x + _dot(a.astype(BF16), wd_ref[...])


def _dense_ffn(x, g, w_gate, w_up, w_down):
    B, S, D = x.shape
    Fd = w_gate.shape[1]
    tm = WIDE_TILE
    return pl.pallas_call(
        _ffn_kernel,
        grid=(B, S // tm),
        in_specs=[
            pl.BlockSpec((1, tm, D), lambda b, s: (b, s, 0)),
            _const_spec((1, D)),
            _const_spec((D, Fd)),
            _const_spec((D, Fd)),
            _const_spec((Fd, D)),
        ],
        out_specs=pl.BlockSpec((1, tm, D), lambda b, s: (b, s, 0)),
        out_shape=jax.ShapeDtypeStruct((B, S, D), F32),
        compiler_params=_params(("arbitrary", "arbitrary")),
        name="dense_ffn",
    )(x, g, w_gate, w_up, w_down)


def _proj_kernel(x_ref, g_ref, w_ref, cos_ref, sin_ref, o0_ref, o1_ref, o2_ref, t_ref, *, rope):
    tm = x_ref.shape[1]
    xn = _rms(x_ref[0], g_ref[...]).astype(BF16)
    out_refs = (o0_ref, o1_ref, o2_ref)
    for j, dil in enumerate(DILATIONS):
        t = _dot(xn, w_ref[:, j * D_MODEL:(j + 1) * D_MODEL])
        for p in range(D_MODEL // LANES):
            cols = slice(p * LANES, (p + 1) * LANES)
            tp = t[:, cols]
            if rope:
                tp = tp * cos_ref[...] + pltpu.roll(tp, LANES // 2, 1) * sin_ref[...]
            if dil == 1:
                out_refs[j][0, 0, :, cols] = tp.astype(BF16)
            else:
                t_ref[p] = tp
                for r in range(dil):
                    out_refs[j][0, r, :, cols] = t_ref[p, pl.ds(r, tm // dil, stride=dil), :].astype(BF16)


def _norm_proj(x, g, w, cos, sin, rope):
    B, S, D = x.shape
    tm = WIDE_TILE
    return pl.pallas_call(
        functools.partial(_proj_kernel, rope=rope),
        grid=(B, S // tm),
        in_specs=[
            pl.BlockSpec((1, tm, D), lambda b, s: (b, s, 0)),
            _const_spec((1, D)),
            _const_spec((D, N_BRANCH * D)),
            pl.BlockSpec((tm, LANES), lambda b, s: (s, 0)),
            pl.BlockSpec((tm, LANES), lambda b, s: (s, 0)),
        ],
        out_specs=[pl.BlockSpec((1, dil, tm // dil, D), lambda b, s: (b, 0, s, 0)) for dil in DILATIONS],
        out_shape=[jax.ShapeDtypeStruct((B, dil, S // dil, D), BF16) for dil in DILATIONS],
        scratch_shapes=[pltpu.VMEM((D // LANES, tm, LANES), F32)],
        compiler_params=_params(("arbitrary", "arbitrary")),
        name="norm_proj_rope" if rope else "norm_proj",
    )(x, g, w, cos, sin)


def _rope_tables(seq_len, scale):
    half = ATT_HEAD_DIM // 2
    inv_freq = ROPE_THETA ** (-jnp.arange(half, dtype=F32) / half)
    ang = jnp.arange(seq_len, dtype=F32)[:, None] * inv_freq[None, :]
    cos = jnp.tile(jnp.cos(ang), (1, LANES // half))
    sin = jnp.tile(jnp.sin(ang), (1, LANES // half))
    sign = jnp.where(jnp.arange(LANES) < LANES // 2, -1.0, 1.0).astype(F32)
    return cos * scale, sin * sign[None, :] * scale


def _pair_block_columns(w):
    rows = w.shape[0]
    half = ATT_HEAD_DIM // 2
    w = w.reshape(rows, -1, 2, 2, half)
    return jnp.swapaxes(w, 2, 3).reshape(rows, -1).astype(BF16)


def _attn_kernel(q_ref, kp_ref, kc_ref, vp_ref, vc_ref, o_ref, lse_ref, s_scr, p_scr, m_scr, l_scr):
    n = pl.program_id(2)
    qi = lax.broadcasted_iota(jnp.int32, (BAND, 2 * BAND), 0)
    ki = lax.broadcasted_iota(jnp.int32, (BAND, 2 * BAND), 1)
    dist = qi + BAND - ki
    first_key = jnp.where(n > 0, 0, BAND)
    valid = (dist >= 0) & (dist <= BAND) & (ki >= first_key)
    valid = jnp.concatenate([valid, valid], axis=0)
    lane = lax.broadcasted_iota(jnp.int32, (BAND, LANES), 1)
    head_lanes = [(lane // (ATT_HEAD_DIM // 2)) % 2 == a for a in range(2)]
    m_scr[...] = jnp.zeros((BAND, LANES), F32)
    l_scr[...] = jnp.ones((BAND, LANES), F32)
    n_pairs = ATT_HEADS // 2
    for p in range(n_pairs):
        cols = slice(p * LANES, (p + 1) * LANES)
        qp = q_ref[:, cols]
        kcat = jnp.concatenate([kp_ref[:, cols], kc_ref[:, cols]], axis=0)
        q2 = jnp.concatenate([jnp.where(head_lanes[a], qp, jnp.zeros_like(qp)) for a in range(2)], axis=0)
        s_scr[p] = jnp.where(valid, _dot_nt(q2, kcat), NEG_BIG)
    inv_l = []
    for p in range(n_pairs):
        s = s_scr[p]
        m = jnp.max(s, axis=-1, keepdims=True)
        pr = jnp.exp2(s - m)
        l = jnp.sum(pr, axis=-1, keepdims=True)
        p_scr[p] = pr.astype(BF16)
        inv_l.append(1.0 / l)
        for a in range(2):
            rows = slice(a * BAND, (a + 1) * BAND)
            m_scr[:, 2 * p + a:2 * p + a + 1] = m[rows]
            l_scr[:, 2 * p + a:2 * p + a + 1] = l[rows]
    for p in range(n_pairs):
        cols = slice(p * LANES, (p + 1) * LANES)
        vcat = jnp.concatenate([vp_ref[:, cols], vc_ref[:, cols]], axis=0)
        o = _dot(p_scr[p], vcat) * inv_l[p]
        o_ref[:, cols] = jnp.where(lane < ATT_HEAD_DIM, o[:BAND], o[BAND:]).astype(o_ref.dtype)
    lse_ref[...] = m_scr[...] * LN2 + jnp.log(l_scr[...])


def _dilated_attn(q, k, v, dil):
    B, _, L, D = q.shape
    nb = L // BAND
    blk = (None, None, BAND, D)
    cur = lambda b, r, n: (b, r, n, 0)
    prev = lambda b, r, n: (b, r, jnp.maximum(n - 1, 0), 0)
    return pl.pallas_call(
        _attn_kernel,
        grid=(B, dil, nb),
        in_specs=[
            pl.BlockSpec(blk, cur),
            pl.BlockSpec(blk, prev),
            pl.BlockSpec(blk, cur),
            pl.BlockSpec(blk, prev),
            pl.BlockSpec(blk, cur),
        ],
        out_specs=[pl.BlockSpec(blk, cur), pl.BlockSpec((None, None, BAND, LANES), cur)],
        out_shape=[
            jax.ShapeDtypeStruct((B, dil, L, D), BF16),
            jax.ShapeDtypeStruct((B, dil, L, LANES), F32),
        ],
        scratch_shapes=[
            pltpu.VMEM((ATT_HEADS // 2, 2 * BAND, 2 * BAND), F32),
            pltpu.VMEM((ATT_HEADS // 2, 2 * BAND, 2 * BAND), BF16),
            pltpu.VMEM((BAND, LANES), F32),
            pltpu.VMEM((BAND, LANES), F32),
        ],
        compiler_params=_params(("arbitrary", "arbitrary", "arbitrary")),
        name=f"dilated_attn_d{dil}",
    )(q, k, k, v, v)


def _combine_kernel(o0_ref, o1_ref, o2_ref, l0_ref, l1_ref, l2_ref, h_ref, wout_ref, expand_ref,
                    g_ref, wr_ref, h_out_ref, hn_out_ref, route_ref, o_scr, l_scr):
    tm = h_ref.shape[1]
    o_refs = (o0_ref, o1_ref, o2_ref)
    l_refs = (l0_ref, l1_ref, l2_ref)
    for i, dil in enumerate(DILATIONS):
        for r in range(dil):
            rows = pl.ds(r, tm // dil, stride=dil)
            l_scr[i, rows, :] = l_refs[i][0, r]
            for p in range(D_MODEL // LANES):
                o_scr[i, p, rows, :] = o_refs[i][0, r, :, p * LANES:(p + 1) * LANES].astype(F32)
    lses = [l_scr[i] for i in range(N_BRANCH)]
    m = jnp.maximum(jnp.maximum(lses[0], lses[1]), lses[2])
    es = [jnp.exp(l - m) for l in lses]
    den = es[0] + es[1] + es[2]
    o = None
    for i in range(N_BRANCH):
        w = es[i] / den
        w_hi = w.astype(BF16)
        w_lo = (w - w_hi.astype(F32)).astype(BF16)
        w_full = _dot(jnp.concatenate([w_hi, w_lo], axis=1), expand_ref[...])
        term = w_full * jnp.concatenate([o_scr[i, p] for p in range(D_MODEL // LANES)], axis=1)
        o = term if o is None else o + term
    h = h_ref[0] + _dot(o.astype(BF16), wout_ref[...])
    h_out_ref[...] = h
    hn = _rms(h, g_ref[...])
    hn_hi = hn.astype(BF16)
    hn_out_ref[...] = hn_hi

    hn_lo = (hn - hn_hi.astype(F32)).astype(BF16)
    l_hi = _dot(hn_hi, wr_ref[...])
    logits = l_hi[:, :LANES] + l_hi[:, LANES:] + _dot(hn_lo, wr_ref[:, :LANES])
    lane = lax.broadcasted_iota(jnp.int32, logits.shape, 1)
    lane_f = lane.astype(F32)
    logits = jnp.where(lane < N_EXPERTS, logits, NEG_BIG)
    mx = jnp.max(logits, axis=-1, keepdims=True)
    pe = jnp.exp(logits - mx)
    probs = pe / jnp.sum(pe, axis=-1, keepdims=True)
    probs = jnp.where(lane < N_EXPERTS, probs, -1.0)
    p1 = jnp.max(probs, axis=-1, keepdims=True)
    i1 = jnp.min(jnp.where(probs == p1, lane_f, float(LANES)), axis=-1, keepdims=True)
    rest = jnp.where(lane_f == i1, -1.0, probs)
    p2 = jnp.max(rest, axis=-1, keepdims=True)
    i2 = jnp.min(jnp.where(rest == p2, lane_f, float(LANES)), axis=-1, keepdims=True)
    tot = p1 + p2
    route = jnp.where(lane_f == i1, p1 / tot, 0.0)
    route = jnp.where(lane_f == i2, p2 / tot, route)
    sel = jnp.where((lane_f == i1 + N_EXPERTS) | (lane_f == i2 + N_EXPERTS), 1.0, 0.0)
    route_ref[...] = route + sel


def _attn_combine(os, lses, h, w_out, expand, g, wr):
    B, S, D = h.shape
    T = B * S
    tm = WIDE_TILE
    ts = S // tm
    row = lambda b, s: (b * ts + s, 0)
    o_specs = [pl.BlockSpec((1, dil, tm // dil, D), lambda b, s: (b, 0, s, 0)) for dil in DILATIONS]
    l_specs = [pl.BlockSpec((1, dil, tm // dil, LANES), lambda b, s: (b, 0, s, 0)) for dil in DILATIONS]
    return pl.pallas_call(
        _combine_kernel,
        grid=(B, ts),
        in_specs=o_specs + l_specs + [
            pl.BlockSpec((1, tm, D), lambda b, s: (b, s, 0)),
            _const_spec((D, D)),
            _const_spec((2 * LANES, D)),
            _const_spec((1, D)),
            _const_spec((D, 2 * LANES)),
        ],
        out_specs=[pl.BlockSpec((tm, D), row), pl.BlockSpec((tm, D), row), pl.BlockSpec((tm, LANES), row)],
        out_shape=[
            jax.ShapeDtypeStruct((T, D), F32),
            jax.ShapeDtypeStruct((T, D), BF16),
            jax.ShapeDtypeStruct((T, LANES), F32),
        ],
        scratch_shapes=[pltpu.VMEM((N_BRANCH, D // LANES, tm, LANES), F32),
                        pltpu.VMEM((N_BRANCH, tm, LANES), F32)],
        compiler_params=_params(("arbitrary", "arbitrary")),
        name="attn_combine_route",
    )(*os, *lses, h, w_out, expand, g, wr)


SHORT_BLOCK = 96


def _spread_tables(n_slots):
    n_cols = N_EXPERTS * n_slots
    col_expert = jnp.arange(n_cols, dtype=jnp.int32) // n_slots
    src = jnp.arange(LANES, dtype=jnp.int32)
    spread = (src[:, None] == col_expert[None, :] + N_EXPERTS).astype(BF16)
    slots = (jnp.arange(n_cols, dtype=jnp.int32) % n_slots).astype(F32).reshape(1, n_cols)
    return jnp.concatenate([spread, spread], axis=0), slots


def _slot_one_hot(route, spread_ref, slots_ref, weighted=False):
    tm = route.shape[0]
    lane = lax.broadcasted_iota(jnp.int32, route.shape, 1)
    sel = jnp.where((lane >= N_EXPERTS) & (lane < 2 * N_EXPERTS), route, 0.0)
    r = lax.broadcasted_iota(jnp.int32, (tm, tm), 0)
    c = lax.broadcasted_iota(jnp.int32, (tm, tm), 1)
    strict_lower = jnp.where(c < r, 1.0, 0.0).astype(BF16)
    rank = _dot(strict_lower, sel.astype(BF16))
    key = jnp.where(sel > 0.5, rank, -1.0).astype(BF16)
    hit = _dot(key, spread_ref[:LANES, :]) == slots_ref[...]
    if not weighted:
        return jnp.where(hit, 1.0, 0.0)
    w = pltpu.roll(route, N_EXPERTS, 1)
    w_hi = w.astype(BF16)
    w_lo = (w - w_hi.astype(F32)).astype(BF16)
    return jnp.where(hit, _dot(jnp.concatenate([w_hi, w_lo], axis=1), spread_ref[...]), 0.0)


def _block_copy(src, dst, sem):
    return pltpu.make_async_copy(src, dst, sem)


def _when_short(short_ref, step, fn, tables, tm):
    @pl.when(short_ref[step] > 0)
    def _():
        fn(SHORT_BLOCK, *tables[:2])

    @pl.when(short_ref[step] == 0)
    def _():
        fn(tm, *tables[2:])


def _dispatch_kernel(offs_ref, short_ref, fill_ref, route_ref, hn_ref, sp_s, sl_s, sp_l, sl_l, xs_hbm, stage, sem):
    tables = (sp_s, sl_s, sp_l, sl_l)
    j = pl.program_id(0)
    n = pl.num_programs(0)
    tm = hn_ref.shape[0]
    n_tiles = xs_hbm.shape[0] // tm
    slot = j % 2

    def zero_copy(row):
        return _block_copy(stage.at[1, 0], xs_hbm.at[pl.ds(pl.multiple_of(row, SUBLANES), tm)], sem.at[1])

    @pl.when(j == 0)
    def _():
        stage[1, 0] = jnp.zeros((tm, D_MODEL), F32)
        first_tail = fill_ref[N_EXPERTS]

        def start_tail(t, carry):
            zero_copy(t * tm).start()
            return carry

        def wait_tail(t, carry):
            zero_copy(t * tm).wait()
            return carry

        first = [zero_copy(fill_ref[e]) for e in range(N_EXPERTS)]
        second = [zero_copy(fill_ref[e] + tm) for e in range(N_EXPERTS)]
        for cp in first:
            cp.start()
        lax.fori_loop(first_tail, n_tiles, start_tail, 0)
        for cp in first:
            cp.wait()
        lax.fori_loop(first_tail, n_tiles, wait_tail, 0)
        for cp in second:
            cp.start()
        for cp in second:
            cp.wait()

    def copies(step_slot, step, n_rows):
        return [_block_copy(stage.at[step_slot, e, pl.ds(0, n_rows)],
                            xs_hbm.at[pl.ds(pl.multiple_of(offs_ref[step * N_EXPERTS + e], SUBLANES), n_rows)],
                            sem.at[step_slot])
                for e in range(N_EXPERTS)]

    def place(n_rows, spread_ref, slots_ref):
        one_hot = _slot_one_hot(route_ref[...], spread_ref, slots_ref).astype(BF16)
        placed = _dot_tn(one_hot, hn_ref[...])
        for e in range(N_EXPERTS):
            stage[slot, e, :n_rows] = placed[e * n_rows:(e + 1) * n_rows]

    _when_short(short_ref, j, place, tables, tm)

    @pl.when(j > 0)
    def _():
        _when_short(short_ref, j - 1, lambda n_rows, *_: [cp.wait() for cp in copies(1 - slot, j - 1, n_rows)],
                    tables, tm)

    _when_short(short_ref, j, lambda n_rows, *_: [cp.start() for cp in copies(slot, j, n_rows)], tables, tm)

    @pl.when(j == n - 1)
    def _():
        _when_short(short_ref, j, lambda n_rows, *_: [cp.wait() for cp in copies(slot, j, n_rows)], tables, tm)


def _table_specs(tm):
    return [_const_spec((2 * LANES, N_EXPERTS * SHORT_BLOCK)), _const_spec((1, N_EXPERTS * SHORT_BLOCK)),
            _const_spec((2 * LANES, N_EXPERTS * tm)), _const_spec((1, N_EXPERTS * tm))]


def _moe_dispatch(route, hn, offs, short, fill, tables, n_rows):
    T, D = hn.shape
    tm = TOKEN_TILE
    grid_spec = pltpu.PrefetchScalarGridSpec(
        num_scalar_prefetch=3,
        grid=(T // tm,),
        in_specs=[
            pl.BlockSpec((tm, LANES), lambda j, *_: (j, 0)),
            pl.BlockSpec((tm, D), lambda j, *_: (j, 0)),
        ] + _table_specs(tm),
        out_specs=pl.BlockSpec(memory_space=pl.ANY),
        scratch_shapes=[pltpu.VMEM((2, N_EXPERTS, tm, D), F32), pltpu.SemaphoreType.DMA((2,))],
    )
    return pl.pallas_call(
        _dispatch_kernel,
        grid_spec=grid_spec,
        out_shape=jax.ShapeDtypeStruct((n_rows, D), F32),
        compiler_params=_params(("arbitrary",)),
        name="moe_dispatch",
    )(offs, short, fill, route, hn, *tables)


EXPERT_F_CHUNK = 512


def _expert_kernel(te_ref, xb_ref, act_ref, first_ref, x_ref, wg_hbm, wu_hbm, wd_hbm, y_ref,
                   wg_buf, wu_buf, wd_buf, stg_gu, stg_d, sem):
    del xb_ref
    i = pl.program_id(0)
    tm = x_ref.shape[0]
    fc = EXPERT_F_CHUNK
    n_chunks = wg_buf.shape[1] // fc

    def chunk_copies(e, c, slot):
        cols = pl.ds(c * fc, fc)
        return [pltpu.make_async_copy(wg_hbm.at[e, :, cols], stg_gu.at[slot, 0], sem.at[slot]),
                pltpu.make_async_copy(wu_hbm.at[e, :, cols], stg_gu.at[slot, 1], sem.at[slot]),
                pltpu.make_async_copy(wd_hbm.at[e, cols, :], stg_d.at[slot], sem.at[slot])]

    def convert_chunk(c, slot):
        cs = slice(c * fc, (c + 1) * fc)
        wg_buf[:, cs] = stg_gu[slot, 0].astype(BF16)
        wu_buf[:, cs] = stg_gu[slot, 1].astype(BF16)
        wd_buf[cs, :] = stg_d[slot].astype(BF16)

    def swiglu_tile(load):
        x = x_ref[...].astype(BF16)
        acc = jnp.zeros((tm, D_MODEL), F32)
        if load is not None:
            for cp in chunk_copies(load, 0, 0):
                cp.start()
        for c in range(n_chunks):
            if load is not None:
                if c + 1 < n_chunks:
                    for cp in chunk_copies(load, c + 1, (c + 1) % 2):
                        cp.start()
                for cp in chunk_copies(load, c, c % 2):
                    cp.wait()
                convert_chunk(c, c % 2)
            cs = slice(c * fc, (c + 1) * fc)
            a = _silu(_dot(x, wg_buf[:, cs])) * _dot(x, wu_buf[:, cs])
            acc = acc + _dot(a.astype(BF16), wd_buf[cs, :])
        y_ref[...] = acc

    @pl.when((act_ref[i] > 0) & (first_ref[i] > 0))
    def _():
        swiglu_tile(te_ref[i])

    @pl.when((act_ref[i] > 0) & (first_ref[i] == 0))
    def _():
        swiglu_tile(None)

    @pl.when(act_ref[i] == 0)
    def _():
        y_ref[...] = jnp.zeros_like(y_ref)


def _moe_experts(xs, tile_expert, x_block, active, first, w_gate, w_up, w_down):
    n_rows, D = xs.shape
    Fd = w_gate.shape[2]
    tm = MOE_ROW_TILE
    fc = EXPERT_F_CHUNK
    G = n_rows // tm
    grid_spec = pltpu.PrefetchScalarGridSpec(
        num_scalar_prefetch=4,
        grid=(G,),
        in_specs=[
            pl.BlockSpec((tm, D), lambda i, te, xb, *_: (xb[i], 0)),
            pl.BlockSpec(memory_space=pl.ANY),
            pl.BlockSpec(memory_space=pl.ANY),
            pl.BlockSpec(memory_space=pl.ANY),
        ],
        out_specs=pl.BlockSpec((tm, D), lambda i, *_: (i, 0)),
        scratch_shapes=[
            pltpu.VMEM((D, Fd), BF16),
            pltpu.VMEM((D, Fd), BF16),
            pltpu.VMEM((Fd, D), BF16),
            pltpu.VMEM((2, 2, D, fc), F32),
            pltpu.VMEM((2, fc, D), F32),
            pltpu.SemaphoreType.DMA((2,)),
        ],
    )
    return pl.pallas_call(
        _expert_kernel,
        grid_spec=grid_spec,
        out_shape=jax.ShapeDtypeStruct((n_rows, D), F32),
        compiler_params=_params(("arbitrary",)),
        name="moe_experts",
    )(tile_expert, x_block, active, first, xs, w_gate, w_up, w_down)


def _finish_kernel(offs_ref, short_ref, route_ref, h_ref, g_ref, sp_s, sl_s, sp_l, sl_l, ys_hbm, out_ref, ybuf, sem):
    tables = (sp_s, sl_s, sp_l, sl_l)
    j = pl.program_id(0)
    n = pl.num_programs(0)
    tm = h_ref.shape[0]
    slot = j % 2

    def copies(step_slot, step, n_rows):
        return [_block_copy(ys_hbm.at[pl.ds(pl.multiple_of(offs_ref[step * N_EXPERTS + e], SUBLANES), n_rows)],
                            ybuf.at[step_slot, e, pl.ds(0, n_rows)], sem.at[step_slot])
                for e in range(N_EXPERTS)]

    @pl.when(j == 0)
    def _():
        _when_short(short_ref, 0, lambda n_rows, *_: [cp.start() for cp in copies(0, 0, n_rows)], tables, tm)

    @pl.when(j + 1 < n)
    def _():
        _when_short(short_ref, j + 1, lambda n_rows, *_: [cp.start() for cp in copies(1 - slot, j + 1, n_rows)],
                    tables, tm)

    def gather(n_rows, spread_ref, slots_ref):
        w = _slot_one_hot(route_ref[...], spread_ref, slots_ref, weighted=True)
        w_hi = w.astype(BF16)
        w_lo = (w - w_hi.astype(F32)).astype(BF16)
        for cp in copies(slot, j, n_rows):
            cp.wait()
        y = jnp.concatenate([ybuf[slot, e, :n_rows].astype(BF16) for e in range(N_EXPERTS)], axis=0)
        out_ref[...] = _rms(h_ref[...] + _dot(w_hi, y) + _dot(w_lo, y), g_ref[...])

    _when_short(short_ref, j, gather, tables, tm)


def _moe_finish(ys, offs, short, route, h, g, tables):
    T, D = h.shape
    tm = TOKEN_TILE
    grid_spec = pltpu.PrefetchScalarGridSpec(
        num_scalar_prefetch=2,
        grid=(T // tm,),
        in_specs=[
            pl.BlockSpec((tm, LANES), lambda j, *_: (j, 0)),
            pl.BlockSpec((tm, D), lambda j, *_: (j, 0)),
            pl.BlockSpec((1, D), lambda j, *_: (0, 0)),
        ] + _table_specs(tm) + [
            pl.BlockSpec(memory_space=pl.ANY),
        ],
        out_specs=pl.BlockSpec((tm, D), lambda j, *_: (j, 0)),
        scratch_shapes=[pltpu.VMEM((2, N_EXPERTS, tm, D), F32), pltpu.SemaphoreType.DMA((2,))],
    )
    return pl.pallas_call(
        _finish_kernel,
        grid_spec=grid_spec,
        out_shape=jax.ShapeDtypeStruct((T, D), F32),
        compiler_params=_params(("arbitrary",)),
        name="moe_finish",
    )(offs, short, route, h, g, *tables, ys)


def _route_metadata(route, tm):
    T = route.shape[0]
    nt = T // tm
    sel = (route[:, N_EXPERTS:2 * N_EXPERTS] > 0.5).astype(jnp.int32)
    cnt_tile = sel.reshape(nt, tm, N_EXPERTS).sum(axis=1)
    short = (jnp.max(cnt_tile, axis=1) <= SHORT_BLOCK).astype(jnp.int32)
    cnt_tile = ((cnt_tile + SUBLANES - 1) // SUBLANES) * SUBLANES
    n_e = cnt_tile.sum(axis=0)
    padded = ((n_e + tm - 1) // tm + 1) * tm
    ends = jnp.cumsum(padded)
    starts = ends - padded
    offs = starts[None, :] + jnp.cumsum(cnt_tile, axis=0) - cnt_tile
    fill = starts + n_e
    max_rows = TOP_K * T + nt * N_EXPERTS * (SUBLANES - 1)
    G = -(-max_rows // tm) + 2 * N_EXPERTS + 1
    tile_start = jnp.arange(G, dtype=jnp.int32) * tm
    tile_expert = jnp.minimum(jnp.sum((tile_start[:, None] >= ends[None, :]).astype(jnp.int32), axis=1),
                              N_EXPERTS - 1)
    active = (tile_start < fill[tile_expert]).astype(jnp.int32)
    x_block = jnp.where(active > 0, jnp.arange(G, dtype=jnp.int32), jnp.argmax(active).astype(jnp.int32))
    first = ((active > 0) & (tile_start == starts[tile_expert])).astype(jnp.int32)
    fill = jnp.concatenate([fill, ends[-1:] // tm])
    return (offs.reshape(-1).astype(jnp.int32), short, fill.astype(jnp.int32), tile_expert.astype(jnp.int32),
            x_block, active, first, G * tm)


def kernel(x, attn_norm, ffn_norm, hgrn_w_in, hgrn_lower_bounds, hgrn_out_norm, hgrn_w_out, kv_norm, w_kv,
           dil_w_q, dil_w_out, ffn_w_gate, ffn_w_up, ffn_w_down, moe_w_router, moe_w_gate, moe_w_up,
           moe_w_down, final_norm):
    B, S, D = x.shape
    row = lambda v: v.reshape(1, -1).astype(F32)

    lbs = jnp.cumsum(jax.nn.softmax(hgrn_lower_bounds.astype(F32), axis=0), axis=0)
    h = _hgrn_mixer(x, row(attn_norm[0]), hgrn_w_in[0].astype(BF16), row(lbs[0]), row(hgrn_out_norm[0]),
                    hgrn_w_out[0].astype(BF16))
    h = _dense_ffn(h, row(ffn_norm[0]), ffn_w_gate[0].astype(BF16), ffn_w_up[0].astype(BF16),
                   ffn_w_down[0].astype(BF16))

    cos_k, sin_k = _rope_tables(S, 1.0)
    cos_q, sin_q = _rope_tables(S, ATT_SCALE * LOG2E)
    w_k = _pair_block_columns(w_kv[:, :N_BRANCH * D])
    w_v = w_kv[:, N_BRANCH * D:].astype(BF16)
    w_q = _pair_block_columns(dil_w_q[0])
    ks = _norm_proj(h, row(kv_norm), w_k, cos_k, sin_k, rope=True)
    vs = _norm_proj(h, row(kv_norm), w_v, cos_k, sin_k, rope=False)
    qs = _norm_proj(h, row(attn_norm[1]), w_q, cos_q, sin_q, rope=True)
    os, lses = [], []
    for i, dil in enumerate(DILATIONS):
        o_i, lse_i = _dilated_attn(qs[i], ks[i], vs[i], dil)
        os.append(o_i)
        lses.append(lse_i)

    head_of_col = jnp.arange(D, dtype=jnp.int32) // ATT_HEAD_DIM
    expand = (jnp.arange(LANES, dtype=jnp.int32)[:, None] == head_of_col[None, :]).astype(BF16)
    expand = jnp.concatenate([expand, expand], axis=0)
    wr = jnp.zeros((D, LANES), F32).at[:, :N_EXPERTS].set(moe_w_router[0].astype(F32))
    wr_hi = wr.astype(BF16)
    wr_lo = (wr - wr_hi.astype(F32)).astype(BF16)
    h, hn, route = _attn_combine(os, lses, h, dil_w_out[0].astype(BF16), expand, row(ffn_norm[1]),
                                 jnp.concatenate([wr_hi, wr_lo], axis=1))

    offs, short, fill, tile_expert, x_block, active, first, n_rows = _route_metadata(route, TOKEN_TILE)
    tables = _spread_tables(SHORT_BLOCK) + _spread_tables(TOKEN_TILE)
    xs = _moe_dispatch(route, hn, offs, short, fill, tables, n_rows)
    ys = _moe_experts(xs, tile_expert, x_block, active, first, moe_w_gate[0], moe_w_up[0], moe_w_down[0])
    out = _moe_finish(ys, offs, short, route, h, row(final_norm), tables)
    return out.reshape(B, S, D)
```

```python
import functools

import jax
import jax.numpy as jnp
from jax import lax
from jax.experimental import pallas as pl
from jax.experimental.pallas import tpu as pltpu

F32 = jnp.float32
BF16 = jnp.bfloat16

D_MODEL = 1024
HGRN_HEAD_DIM = 128
HGRN_HEADS = D_MODEL // HGRN_HEAD_DIM
HGRN_CHUNK = 64
ATT_HEAD_DIM = 64
ATT_HEADS = D_MODEL // ATT_HEAD_DIM
DILATIONS = (1, 4, 16)
N_BRANCH = len(DILATIONS)
BAND = 128
ATT_SCALE = ATT_HEAD_DIM ** -0.5
ROPE_THETA = 10000.0
N_EXPERTS = 8
TOP_K = 2
EPS = 1e-6

LANES = 128
SUBLANES = 8
TOKEN_TILE = 256
WIDE_TILE = 512
MOE_ROW_TILE = TOKEN_TILE
VMEM_LIMIT = 56 * 1024 * 1024
NEG_BIG = -1e30
LN2 = 0.6931471805599453
LOG2E = 1.0 / LN2


def _dot(a, b):
    return jnp.dot(a, b, preferred_element_type=F32)


def _dot_nt(a, b):
    return lax.dot_general(a, b, (((1,), (1,)), ((), ())), preferred_element_type=F32)


def _dot_tn(a, b):
    return lax.dot_general(a, b, (((0,), (0,)), ((), ())), preferred_element_type=F32)


def _rms(x, g):
    return x * lax.rsqrt(jnp.mean(x * x, axis=-1, keepdims=True) + EPS) * g


def _sigmoid(x):
    return 1.0 / (1.0 + jnp.exp(-x))


def _silu(x):
    return x * _sigmoid(x)


def _split3(x):
    hi = x.astype(BF16)
    r1 = x - hi.astype(F32)
    mid = r1.astype(BF16)
    lo = (r1 - mid.astype(F32)).astype(BF16)
    return hi, mid, lo


def _const_spec(shape):
    nd = len(shape)
    return pl.BlockSpec(shape, lambda *_: (0,) * nd, pipeline_mode=pl.Buffered(1))


def _params(sem):
    return pltpu.CompilerParams(dimension_semantics=sem, vmem_limit_bytes=VMEM_LIMIT)


def _hgrn_kernel(x_ref, g_ref, win_ref, lb_ref, onorm_ref, wout_ref, out_ref,
                 state_ref, y_ref, o_ref, k_ref, b_ref, oi_ref, s_ref, qb_ref, ke_ref, qd_ref, kd_ref):
    tm = x_ref.shape[1]
    dk = HGRN_HEAD_DIM
    c_len = HGRN_CHUNK

    @pl.when(pl.program_id(1) == 0)
    def _():
        state_ref[...] = jnp.zeros_like(state_ref)

    x = x_ref[0]
    xn = _rms(x, g_ref[...]).astype(BF16)
    y_ref[...] = _dot(xn, win_ref[...])

    r = lax.broadcasted_iota(jnp.int32, (tm, tm), 0)
    c = lax.broadcasted_iota(jnp.int32, (tm, tm), 1)
    tri = jnp.where(c <= r, 1.0, 0.0).astype(BF16)

    lb = lb_ref[...]
    f = lb + (1.0 - lb) * _sigmoid(y_ref[:, D_MODEL:2 * D_MODEL])
    k_ref[...] = 1.0 - f
    lf_hi, lf_mid, lf_lo = _split3(jnp.log(f))
    b_ref[...] = _dot(tri, lf_hi) + _dot(tri, lf_mid) + _dot(tri, lf_lo)

    n_chunks = tm // c_len
    b_last = b_ref[tm - 1:tm, :]
    qb_ref[...] = (y_ref[:, :D_MODEL] * jnp.exp(b_ref[...])).astype(BF16)
    ke_ref[...] = (k_ref[...] * jnp.exp(b_last - b_ref[...])).astype(BF16)
    for ci in range(n_chunks):
        rows = slice(ci * c_len, (ci + 1) * c_len)
        n_keys = (ci + 1) * c_len
        ref = b_ref[ci * c_len - 1:ci * c_len, :] if ci > 0 else jnp.zeros((1, D_MODEL), F32)
        qd_ref[rows, :] = (y_ref[rows, :D_MODEL] * jnp.exp(b_ref[rows, :] - ref)).astype(BF16)
        kd_ref[ci, :n_keys, :] = (k_ref[:n_keys, :] * jnp.exp(ref - b_ref[:n_keys, :])).astype(BF16)
    decay = jnp.exp(b_last)

    for h in range(HGRN_HEADS):
        cols = slice(h * dk, (h + 1) * dk)
        v16 = y_ref[:, 2 * D_MODEL + h * dk:2 * D_MODEL + (h + 1) * dk].astype(BF16)
        st = state_ref[h]
        oi_ref[:, cols] = _dot_nt(qb_ref[:, cols], st.astype(BF16))
        state_ref[h] = st * decay[:, cols] + _dot_tn(v16, ke_ref[:, cols])
        for ci in range(n_chunks):
            rows = slice(ci * c_len, (ci + 1) * c_len)
            n_keys = (ci + 1) * c_len
            qpos = lax.broadcasted_iota(jnp.int32, (c_len, n_keys), 0) + ci * c_len
            kpos = lax.broadcasted_iota(jnp.int32, (c_len, n_keys), 1)
            scores = _dot_nt(qd_ref[rows, cols], kd_ref[ci, :n_keys, cols])
            s_ref[h, rows, :n_keys] = jnp.where(kpos <= qpos, scores, 0.0).astype(BF16)
            if n_keys < tm:
                s_ref[h, rows, n_keys:] = jnp.zeros((c_len, tm - n_keys), BF16)

    for h in range(HGRN_HEADS):
        cols = slice(h * dk, (h + 1) * dk)
        v16 = y_ref[:, 2 * D_MODEL + h * dk:2 * D_MODEL + (h + 1) * dk].astype(BF16)
        gate = y_ref[:, 3 * D_MODEL + h * dk:3 * D_MODEL + (h + 1) * dk]
        o = _dot(s_ref[h], v16) + oi_ref[:, cols]
        o_ref[:, cols] = (_rms(o, onorm_ref[...]) * _silu(gate)).astype(BF16)

    out_ref[0] = x + _dot(o_ref[...], wout_ref[...])


def _hgrn_mixer(x, g, w_in, lb, out_norm, w_out):
    B, S, D = x.shape
    tm = TOKEN_TILE
    return pl.pallas_call(
        _hgrn_kernel,
        grid=(B, S // tm),
        in_specs=[
            pl.BlockSpec((1, tm, D), lambda b, s: (b, s, 0)),
            _const_spec((1, D)),
            _const_spec((D, 4 * D)),
            _const_spec((1, D)),
            _const_spec((1, HGRN_HEAD_DIM)),
            _const_spec((D, D)),
        ],
        out_specs=pl.BlockSpec((1, tm, D), lambda b, s: (b, s, 0)),
        out_shape=jax.ShapeDtypeStruct((B, S, D), F32),
        scratch_shapes=[
            pltpu.VMEM((HGRN_HEADS, HGRN_HEAD_DIM, HGRN_HEAD_DIM), F32),
            pltpu.VMEM((tm, 4 * D), F32),
            pltpu.VMEM((tm, D), BF16),
            pltpu.VMEM((tm, D), F32),
            pltpu.VMEM((tm, D), F32),
            pltpu.VMEM((tm, D), F32),
            pltpu.VMEM((HGRN_HEADS, tm, tm), BF16),
            pltpu.VMEM((tm, D), BF16),
            pltpu.VMEM((tm, D), BF16),
            pltpu.VMEM((tm, D), BF16),
            pltpu.VMEM((tm // HGRN_CHUNK, tm, D), BF16),
        ],
        compiler_params=_params(("arbitrary", "arbitrary")),
        name="hgrn_mixer",
    )(x, g, w_in, lb, out_norm, w_out)


def _ffn_kernel(x_ref, g_ref, wg_ref, wu_ref, wd_ref, out_ref):
    x = x_ref[0]
    xn = _rms(x, g_ref[...]).astype(BF16)
    a = _silu(_dot(xn, wg_ref[...])) * _dot(xn, wu_ref[...])
    out_ref[0] = x + _dot(a.astype(BF16), wd_ref[...])


def _dense_ffn(x, g, w_gate, w_up, w_down):
    B, S, D = x.shape
    Fd = w_gate.shape[1]
    tm = WIDE_TILE
    return pl.pallas_call(
        _ffn_kernel,
        grid=(B, S // tm),
        in_specs=[
            pl.BlockSpec((1, tm, D), lambda b, s: (b, s, 0)),
            _const_spec((1, D)),
            _const_spec((D, Fd)),
            _const_spec((D, Fd)),
            _const_spec((Fd, D)),
        ],
        out_specs=pl.BlockSpec((1, tm, D), lambda b, s: (b, s, 0)),
        out_shape=jax.ShapeDtypeStruct((B, S, D), F32),
        compiler_params=_params(("arbitrary", "arbitrary")),
        name="dense_ffn",
    )(x, g, w_gate, w_up, w_down)


def _proj_kernel(x_ref, g_ref, w_ref, cos_ref, sin_ref, o0_ref, o1_ref, o2_ref, t_ref, *, rope):
    tm = x_ref.shape[1]
    xn = _rms(x_ref[0], g_ref[...]).astype(BF16)
    out_refs = (o0_ref, o1_ref, o2_ref)
    for j, dil in reversed(list(enumerate(DILATIONS))):
        t = _dot(xn, w_ref[:, j * D_MODEL:(j + 1) * D_MODEL])
        for p in range(D_MODEL // LANES):
            cols = slice(p * LANES, (p + 1) * LANES)
            tp = t[:, cols]
            if rope:
                tp = tp * cos_ref[...] + pltpu.roll(tp, LANES // 2, 1) * sin_ref[...]
            if dil == 1:
                out_refs[j][0, 0, :, cols] = tp.astype(BF16)
            else:
                t_ref[j - 1, p] = tp
                for r in range(dil):
                    out_refs[j][0, r, :, cols] = t_ref[j - 1, p, pl.ds(r, tm // dil, stride=dil), :].astype(BF16)


def _norm_proj(x, g, w, cos, sin, rope):
    B, S, D = x.shape
    tm = WIDE_TILE
    return pl.pallas_call(
        functools.partial(_proj_kernel, rope=rope),
        grid=(B, S // tm),
        in_specs=[
            pl.BlockSpec((1, tm, D), lambda b, s: (b, s, 0)),
            _const_spec((1, D)),
            _const_spec((D, N_BRANCH * D)),
            pl.BlockSpec((tm, LANES), lambda b, s: (s, 0)),
            pl.BlockSpec((tm, LANES), lambda b, s: (s, 0)),
        ],
        out_specs=[pl.BlockSpec((1, dil, tm // dil, D), lambda b, s: (b, 0, s, 0)) for dil in DILATIONS],
        out_shape=[jax.ShapeDtypeStruct((B, dil, S // dil, D), BF16) for dil in DILATIONS],
        scratch_shapes=[pltpu.VMEM((N_BRANCH - 1, D // LANES, tm, LANES), F32)],
        compiler_params=_params(("arbitrary", "arbitrary")),
        name="norm_proj_rope" if rope else "norm_proj",
    )(x, g, w, cos, sin)


def _rope_tables(seq_len, scale):
    half = ATT_HEAD_DIM // 2
    inv_freq = ROPE_THETA ** (-jnp.arange(half, dtype=F32) / half)
    ang = jnp.arange(seq_len, dtype=F32)[:, None] * inv_freq[None, :]
    cos = jnp.tile(jnp.cos(ang), (1, LANES // half))
    sin = jnp.tile(jnp.sin(ang), (1, LANES // half))
    sign = jnp.where(jnp.arange(LANES) < LANES // 2, -1.0, 1.0).astype(F32)
    return cos * scale, sin * sign[None, :] * scale


def _pair_block_columns(w):
    rows = w.shape[0]
    half = ATT_HEAD_DIM // 2
    w = w.reshape(rows, -1, 2, 2, half)
    return jnp.swapaxes(w, 2, 3).reshape(rows, -1).astype(BF16)


ATT_BLOCKS_PER_STEP = 2


def _attn_kernel(q_ref, kp_ref, kc_ref, vp_ref, vc_ref, o_ref, lse_ref, s_scr, p_scr, m_scr, l_scr):
    n = pl.program_id(2)
    n_blocks = q_ref.shape[0] // BAND
    qi = lax.broadcasted_iota(jnp.int32, (BAND, 2 * BAND), 0)
    ki = lax.broadcasted_iota(jnp.int32, (BAND, 2 * BAND), 1)
    dist = qi + BAND - ki
    band = (dist >= 0) & (dist <= BAND)
    first_key = jnp.where(n > 0, 0, BAND)
    band = jnp.concatenate([band, band], axis=0)
    kpos = jnp.concatenate([ki, ki], axis=0)
    lane = lax.broadcasted_iota(jnp.int32, (BAND, LANES), 1)
    head_lanes = [(lane // (ATT_HEAD_DIM // 2)) % 2 == a for a in range(2)]
    m_scr[...] = jnp.zeros(m_scr.shape, F32)
    l_scr[...] = jnp.ones(l_scr.shape, F32)
    n_pairs = ATT_HEADS // 2
    units = [(j, p) for j in range(n_blocks) for p in range(n_pairs)]

    def prev_and_cur(prev_ref, cur_ref, j, cols):
        prev = prev_ref[:, cols] if j == 0 else cur_ref[(j - 1) * BAND:j * BAND, cols]
        return jnp.concatenate([prev, cur_ref[j * BAND:(j + 1) * BAND, cols]], axis=0)

    for u, (j, p) in enumerate(units):
        cols = slice(p * LANES, (p + 1) * LANES)
        qp = q_ref[j * BAND:(j + 1) * BAND, cols]
        q2 = jnp.concatenate([jnp.where(head_lanes[a], qp, jnp.zeros_like(qp)) for a in range(2)], axis=0)
        valid = band & (kpos >= first_key) if j == 0 else band
        s_scr[u] = jnp.where(valid, _dot_nt(q2, prev_and_cur(kp_ref, kc_ref, j, cols)), NEG_BIG)
    inv_l = []
    for u, (j, p) in enumerate(units):
        s = s_scr[u]
        m = jnp.max(s, axis=-1, keepdims=True)
        pr = jnp.exp2(s - m)
        l = jnp.sum(pr, axis=-1, keepdims=True)
        p_scr[u] = pr.astype(BF16)
        inv_l.append(1.0 / l)
        for a in range(2):
            rows = slice(a * BAND, (a + 1) * BAND)
            m_scr[j * BAND:(j + 1) * BAND, 2 * p + a:2 * p + a + 1] = m[rows]
            l_scr[j * BAND:(j + 1) * BAND, 2 * p + a:2 * p + a + 1] = l[rows]
    for u, (j, p) in enumerate(units):
        cols = slice(p * LANES, (p + 1) * LANES)
        o = _dot(p_scr[u], prev_and_cur(vp_ref, vc_ref, j, cols)) * inv_l[u]
        o_ref[j * BAND:(j + 1) * BAND, cols] = jnp.where(lane < ATT_HEAD_DIM, o[:BAND], o[BAND:]).astype(o_ref.dtype)
    lse_ref[...] = m_scr[...] * LN2 + jnp.log(l_scr[...])


def _dilated_attn(q, k, v, dil):
    B, _, L, D = q.shape
    qb = ATT_BLOCKS_PER_STEP
    rows = qb * BAND
    n_units = qb * (ATT_HEADS // 2)
    blk = (None, None, rows, D)
    cur = lambda b, r, n: (b, r, n, 0)
    prev = lambda b, r, n: (b, r, jnp.maximum(n * qb - 1, 0), 0)
    return pl.pallas_call(
        _attn_kernel,
        grid=(B, dil, L // rows),
        in_specs=[
            pl.BlockSpec(blk, cur),
            pl.BlockSpec((None, None, BAND, D), prev),
            pl.BlockSpec(blk, cur),
            pl.BlockSpec((None, None, BAND, D), prev),
            pl.BlockSpec(blk, cur),
        ],
        out_specs=[pl.BlockSpec(blk, cur), pl.BlockSpec((None, None, rows, LANES), cur)],
        out_shape=[
            jax.ShapeDtypeStruct((B, dil, L, D), BF16),
            jax.ShapeDtypeStruct((B, dil, L, LANES), F32),
        ],
        scratch_shapes=[
            pltpu.VMEM((n_units, 2 * BAND, 2 * BAND), F32),
            pltpu.VMEM((n_units, 2 * BAND, 2 * BAND), BF16),
            pltpu.VMEM((rows, LANES), F32),
            pltpu.VMEM((rows, LANES), F32),
        ],
        compiler_params=_params(("arbitrary", "arbitrary", "arbitrary")),
        name=f"dilated_attn_d{dil}",
    )(q, k, k, v, v)


def _combine_kernel(o0_ref, o1_ref, o2_ref, l0_ref, l1_ref, l2_ref, h_ref, wout_ref, expand_ref,
                    g_ref, wr_ref, h_out_ref, hn_out_ref, route_ref, o_scr, l_scr):
    tm = h_ref.shape[1]
    o_refs = (o0_ref, o1_ref, o2_ref)
    l_refs = (l0_ref, l1_ref, l2_ref)
    for i, dil in enumerate(DILATIONS):
        for r in range(dil):
            rows = pl.ds(r, tm // dil, stride=dil)
            l_scr[i, rows, :] = l_refs[i][0, r]
            for p in range(D_MODEL // LANES):
                o_scr[i, p, rows, :] = o_refs[i][0, r, :, p * LANES:(p + 1) * LANES].astype(F32)
    lses = [l_scr[i] for i in range(N_BRANCH)]
    m = jnp.maximum(jnp.maximum(lses[0], lses[1]), lses[2])
    es = [jnp.exp(l - m) for l in lses]
    den = es[0] + es[1] + es[2]
    o = None
    for i in range(N_BRANCH):
        w = es[i] / den
        w_hi = w.astype(BF16)
        w_lo = (w - w_hi.astype(F32)).astype(BF16)
        w_full = _dot(jnp.concatenate([w_hi, w_lo], axis=1), expand_ref[...])
        term = w_full * jnp.concatenate([o_scr[i, p] for p in range(D_MODEL // LANES)], axis=1)
        o = term if o is None else o + term
    h = h_ref[0] + _dot(o.astype(BF16), wout_ref[...])
    h_out_ref[...] = h
    hn = _rms(h, g_ref[...])
    hn_hi = hn.astype(BF16)
    hn_out_ref[...] = hn_hi

    hn_lo = (hn - hn_hi.astype(F32)).astype(BF16)
    l_hi = _dot(hn_hi, wr_ref[...])
    logits = l_hi[:, :LANES] + l_hi[:, LANES:] + _dot(hn_lo, wr_ref[:, :LANES])
    lane = lax.broadcasted_iota(jnp.int32, logits.shape, 1)
    lane_f = lane.astype(F32)
    logits = jnp.where(lane < N_EXPERTS, logits, NEG_BIG)
    mx = jnp.max(logits, axis=-1, keepdims=True)
    pe = jnp.exp(logits - mx)
    probs = pe / jnp.sum(pe, axis=-1, keepdims=True)
    probs = jnp.where(lane < N_EXPERTS, probs, -1.0)
    p1 = jnp.max(probs, axis=-1, keepdims=True)
    i1 = jnp.min(jnp.where(probs == p1, lane_f, float(LANES)), axis=-1, keepdims=True)
    rest = jnp.where(lane_f == i1, -1.0, probs)
    p2 = jnp.max(rest, axis=-1, keepdims=True)
    i2 = jnp.min(jnp.where(rest == p2, lane_f, float(LANES)), axis=-1, keepdims=True)
    tot = p1 + p2
    route = jnp.where(lane_f == i1, p1 / tot, 0.0)
    route = jnp.where(lane_f == i2, p2 / tot, route)
    sel = jnp.where((lane_f == i1 + N_EXPERTS) | (lane_f == i2 + N_EXPERTS), 1.0, 0.0)
    route_ref[...] = route + sel


def _attn_combine(os, lses, h, w_out, expand, g, wr):
    B, S, D = h.shape
    T = B * S
    tm = WIDE_TILE
    ts = S // tm
    row = lambda b, s: (b * ts + s, 0)
    o_specs = [pl.BlockSpec((1, dil, tm // dil, D), lambda b, s: (b, 0, s, 0)) for dil in DILATIONS]
    l_specs = [pl.BlockSpec((1, dil, tm // dil, LANES), lambda b, s: (b, 0, s, 0)) for dil in DILATIONS]
    return pl.pallas_call(
        _combine_kernel,
        grid=(B, ts),
        in_specs=o_specs + l_specs + [
            pl.BlockSpec((1, tm, D), lambda b, s: (b, s, 0)),
            _const_spec((D, D)),
            _const_spec((2 * LANES, D)),
            _const_spec((1, D)),
            _const_spec((D, 2 * LANES)),
        ],
        out_specs=[pl.BlockSpec((tm, D), row), pl.BlockSpec((tm, D), row), pl.BlockSpec((tm, LANES), row)],
        out_shape=[
            jax.ShapeDtypeStruct((T, D), F32),
            jax.ShapeDtypeStruct((T, D), BF16),
            jax.ShapeDtypeStruct((T, LANES), F32),
        ],
        scratch_shapes=[pltpu.VMEM((N_BRANCH, D // LANES, tm, LANES), F32),
                        pltpu.VMEM((N_BRANCH, tm, LANES), F32)],
        compiler_params=_params(("arbitrary", "arbitrary")),
        name="attn_combine_route",
    )(*os, *lses, h, w_out, expand, g, wr)


SHORT_BLOCK = 96


def _spread_tables(n_slots):
    n_cols = N_EXPERTS * n_slots
    col_expert = jnp.arange(n_cols, dtype=jnp.int32) // n_slots
    src = jnp.arange(LANES, dtype=jnp.int32)
    spread = (src[:, None] == col_expert[None, :] + N_EXPERTS).astype(BF16)
    slots = (jnp.arange(n_cols, dtype=jnp.int32) % n_slots).astype(F32).reshape(1, n_cols)
    return jnp.concatenate([spread, spread], axis=0), slots


def _slot_one_hot(route, spread_ref, slots_ref, weighted=False):
    tm = route.shape[0]
    lane = lax.broadcasted_iota(jnp.int32, route.shape, 1)
    sel = jnp.where((lane >= N_EXPERTS) & (lane < 2 * N_EXPERTS), route, 0.0)
    r = lax.broadcasted_iota(jnp.int32, (tm, tm), 0)
    c = lax.broadcasted_iota(jnp.int32, (tm, tm), 1)
    strict_lower = jnp.where(c < r, 1.0, 0.0).astype(BF16)
    rank = _dot(strict_lower, sel.astype(BF16))
    key = jnp.where(sel > 0.5, rank, -1.0).astype(BF16)
    hit = _dot(key, spread_ref[:LANES, :]) == slots_ref[...]
    if not weighted:
        return jnp.where(hit, 1.0, 0.0)
    w = pltpu.roll(route, N_EXPERTS, 1)
    w_hi = w.astype(BF16)
    w_lo = (w - w_hi.astype(F32)).astype(BF16)
    return jnp.where(hit, _dot(jnp.concatenate([w_hi, w_lo], axis=1), spread_ref[...]), 0.0)


def _block_copy(src, dst, sem):
    return pltpu.make_async_copy(src, dst, sem)


def _when_short(short_ref, step, fn, tables, tm):
    @pl.when(short_ref[step] > 0)
    def _():
        fn(SHORT_BLOCK, *tables[:2])

    @pl.when(short_ref[step] == 0)
    def _():
        fn(tm, *tables[2:])


def _dispatch_kernel(offs_ref, short_ref, fill_ref, route_ref, hn_ref, sp_s, sl_s, sp_l, sl_l, xs_hbm, stage, sem):
    tables = (sp_s, sl_s, sp_l, sl_l)
    j = pl.program_id(0)
    n = pl.num_programs(0)
    tm = hn_ref.shape[0]
    n_tiles = xs_hbm.shape[0] // tm
    slot = j % 2

    def zero_copy(row):
        return _block_copy(stage.at[1, 0], xs_hbm.at[pl.ds(pl.multiple_of(row, SUBLANES), tm)], sem.at[1])

    @pl.when(j == 0)
    def _():
        stage[1, 0] = jnp.zeros((tm, D_MODEL), F32)
        first_tail = fill_ref[N_EXPERTS]

        def start_tail(t, carry):
            zero_copy(t * tm).start()
            return carry

        def wait_tail(t, carry):
            zero_copy(t * tm).wait()
            return carry

        first = [zero_copy(fill_ref[e]) for e in range(N_EXPERTS)]
        second = [zero_copy(fill_ref[e] + tm) for e in range(N_EXPERTS)]
        for cp in first:
            cp.start()
        lax.fori_loop(first_tail, n_tiles, start_tail, 0)
        for cp in first:
            cp.wait()
        lax.fori_loop(first_tail, n_tiles, wait_tail, 0)
        for cp in second:
            cp.start()
        for cp in second:
            cp.wait()

    def copies(step_slot, step, n_rows):
        return [_block_copy(stage.at[step_slot, e, pl.ds(0, n_rows)],
                            xs_hbm.at[pl.ds(pl.multiple_of(offs_ref[step * N_EXPERTS + e], SUBLANES), n_rows)],
                            sem.at[step_slot])
                for e in range(N_EXPERTS)]

    def place(n_rows, spread_ref, slots_ref):
        one_hot = _slot_one_hot(route_ref[...], spread_ref, slots_ref).astype(BF16)
        placed = _dot_tn(one_hot, hn_ref[...])
        for e in range(N_EXPERTS):
            stage[slot, e, :n_rows] = placed[e * n_rows:(e + 1) * n_rows]

    _when_short(short_ref, j, place, tables, tm)

    @pl.when(j > 0)
    def _():
        _when_short(short_ref, j - 1, lambda n_rows, *_: [cp.wait() for cp in copies(1 - slot, j - 1, n_rows)],
                    tables, tm)

    _when_short(short_ref, j, lambda n_rows, *_: [cp.start() for cp in copies(slot, j, n_rows)], tables, tm)

    @pl.when(j == n - 1)
    def _():
        _when_short(short_ref, j, lambda n_rows, *_: [cp.wait() for cp in copies(slot, j, n_rows)], tables, tm)


def _table_specs(tm):
    return [_const_spec((2 * LANES, N_EXPERTS * SHORT_BLOCK)), _const_spec((1, N_EXPERTS * SHORT_BLOCK)),
            _const_spec((2 * LANES, N_EXPERTS * tm)), _const_spec((1, N_EXPERTS * tm))]


def _moe_dispatch(route, hn, offs, short, fill, tables, n_rows):
    T, D = hn.shape
    tm = TOKEN_TILE
    grid_spec = pltpu.PrefetchScalarGridSpec(
        num_scalar_prefetch=3,
        grid=(T // tm,),
        in_specs=[
            pl.BlockSpec((tm, LANES), lambda j, *_: (j, 0)),
            pl.BlockSpec((tm, D), lambda j, *_: (j, 0)),
        ] + _table_specs(tm),
        out_specs=pl.BlockSpec(memory_space=pl.ANY),
        scratch_shapes=[pltpu.VMEM((2, N_EXPERTS, tm, D), F32), pltpu.SemaphoreType.DMA((2,))],
    )
    return pl.pallas_call(
        _dispatch_kernel,
        grid_spec=grid_spec,
        out_shape=jax.ShapeDtypeStruct((n_rows, D), F32),
        compiler_params=_params(("arbitrary",)),
        name="moe_dispatch",
    )(offs, short, fill, route, hn, *tables)


EXPERT_F_CHUNK = 512


def _expert_kernel(te_ref, xb_ref, act_ref, first_ref, x_ref, wg_hbm, wu_hbm, wd_hbm, y_ref,
                   wg_buf, wu_buf, wd_buf, stg_gu, stg_d, sem):
    del xb_ref
    i = pl.program_id(0)
    tm = x_ref.shape[0]
    fc = EXPERT_F_CHUNK
    n_chunks = wg_buf.shape[1] // fc

    def chunk_copies(e, c, slot):
        cols = pl.ds(c * fc, fc)
        return [pltpu.make_async_copy(wg_hbm.at[e, :, cols], stg_gu.at[slot, 0], sem.at[slot]),
                pltpu.make_async_copy(wu_hbm.at[e, :, cols], stg_gu.at[slot, 1], sem.at[slot]),
                pltpu.make_async_copy(wd_hbm.at[e, cols, :], stg_d.at[slot], sem.at[slot])]

    def convert_chunk(c, slot):
        cs = slice(c * fc, (c + 1) * fc)
        wg_buf[:, cs] = stg_gu[slot, 0].astype(BF16)
        wu_buf[:, cs] = stg_gu[slot, 1].astype(BF16)
        wd_buf[cs, :] = stg_d[slot].astype(BF16)

    def swiglu_tile(load):
        x = x_ref[...].astype(BF16)
        acc = jnp.zeros((tm, D_MODEL), F32)
        if load is not None:
            for cp in chunk_copies(load, 0, 0):
                cp.start()
        for c in range(n_chunks):
            if load is not None:
                if c + 1 < n_chunks:
                    for cp in chunk_copies(load, c + 1, (c + 1) % 2):
                        cp.start()
                for cp in chunk_copies(load, c, c % 2):
                    cp.wait()
                convert_chunk(c, c % 2)
            cs = slice(c * fc, (c + 1) * fc)
            a = _silu(_dot(x, wg_buf[:, cs])) * _dot(x, wu_buf[:, cs])
            acc = acc + _dot(a.astype(BF16), wd_buf[cs, :])
        y_ref[...] = acc

    @pl.when((act_ref[i] > 0) & (first_ref[i] > 0))
    def _():
        swiglu_tile(te_ref[i])

    @pl.when((act_ref[i] > 0) & (first_ref[i] == 0))
    def _():
        swiglu_tile(None)

    @pl.when(act_ref[i] == 0)
    def _():
        y_ref[...] = jnp.zeros_like(y_ref)


def _moe_experts(xs, tile_expert, x_block, active, first, w_gate, w_up, w_down):
    n_rows, D = xs.shape
    Fd = w_gate.shape[2]
    tm = MOE_ROW_TILE
    fc = EXPERT_F_CHUNK
    G = n_rows // tm
    grid_spec = pltpu.PrefetchScalarGridSpec(
        num_scalar_prefetch=4,
        grid=(G,),
        in_specs=[
            pl.BlockSpec((tm, D), lambda i, te, xb, *_: (xb[i], 0)),
            pl.BlockSpec(memory_space=pl.ANY),
            pl.BlockSpec(memory_space=pl.ANY),
            pl.BlockSpec(memory_space=pl.ANY),
        ],
        out_specs=pl.BlockSpec((tm, D), lambda i, *_: (i, 0)),
        scratch_shapes=[
            pltpu.VMEM((D, Fd), BF16),
            pltpu.VMEM((D, Fd), BF16),
            pltpu.VMEM((Fd, D), BF16),
            pltpu.VMEM((2, 2, D, fc), F32),
            pltpu.VMEM((2, fc, D), F32),
            pltpu.SemaphoreType.DMA((2,)),
        ],
    )
    return pl.pallas_call(
        _expert_kernel,
        grid_spec=grid_spec,
        out_shape=jax.ShapeDtypeStruct((n_rows, D), F32),
        compiler_params=_params(("arbitrary",)),
        name="moe_experts",
    )(tile_expert, x_block, active, first, xs, w_gate, w_up, w_down)


def _finish_kernel(offs_ref, short_ref, route_ref, h_ref, g_ref, sp_s, sl_s, sp_l, sl_l, ys_hbm, out_ref, ybuf, sem):
    tables = (sp_s, sl_s, sp_l, sl_l)
    j = pl.program_id(0)
    n = pl.num_programs(0)
    tm = h_ref.shape[0]
    slot = j % 2

    def copies(step_slot, step, n_rows):
        return [_block_copy(ys_hbm.at[pl.ds(pl.multiple_of(offs_ref[step * N_EXPERTS + e], SUBLANES), n_rows)],
                            ybuf.at[step_slot, e, pl.ds(0, n_rows)], sem.at[step_slot])
                for e in range(N_EXPERTS)]

    @pl.when(j == 0)
    def _():
        _when_short(short_ref, 0, lambda n_rows, *_: [cp.start() for cp in copies(0, 0, n_rows)], tables, tm)

    @pl.when(j + 1 < n)
    def _():
        _when_short(short_ref, j + 1, lambda n_rows, *_: [cp.start() for cp in copies(1 - slot, j + 1, n_rows)],
                    tables, tm)

    def gather(n_rows, spread_ref, slots_ref):
        w = _slot_one_hot(route_ref[...], spread_ref, slots_ref, weighted=True)
        w_hi = w.astype(BF16)
        w_lo = (w - w_hi.astype(F32)).astype(BF16)
        for cp in copies(slot, j, n_rows):
            cp.wait()
        y = jnp.concatenate([ybuf[slot, e, :n_rows].astype(BF16) for e in range(N_EXPERTS)], axis=0)
        out_ref[...] = _rms(h_ref[...] + _dot(w_hi, y) + _dot(w_lo, y), g_ref[...])

    _when_short(short_ref, j, gather, tables, tm)


def _moe_finish(ys, offs, short, route, h, g, tables):
    T, D = h.shape
    tm = TOKEN_TILE
    grid_spec = pltpu.PrefetchScalarGridSpec(
        num_scalar_prefetch=2,
        grid=(T // tm,),
        in_specs=[
            pl.BlockSpec((tm, LANES), lambda j, *_: (j, 0)),
            pl.BlockSpec((tm, D), lambda j, *_: (j, 0)),
            pl.BlockSpec((1, D), lambda j, *_: (0, 0)),
        ] + _table_specs(tm) + [
            pl.BlockSpec(memory_space=pl.ANY),
        ],
        out_specs=pl.BlockSpec((tm, D), lambda j, *_: (j, 0)),
        scratch_shapes=[pltpu.VMEM((2, N_EXPERTS, tm, D), F32), pltpu.SemaphoreType.DMA((2,))],
    )
    return pl.pallas_call(
        _finish_kernel,
        grid_spec=grid_spec,
        out_shape=jax.ShapeDtypeStruct((T, D), F32),
        compiler_params=_params(("arbitrary",)),
        name="moe_finish",
    )(offs, short, route, h, g, *tables, ys)


def _route_metadata(route, tm):
    T = route.shape[0]
    nt = T // tm
    sel = (route[:, N_EXPERTS:2 * N_EXPERTS] > 0.5).astype(jnp.int32)
    cnt_tile = sel.reshape(nt, tm, N_EXPERTS).sum(axis=1)
    short = (jnp.max(cnt_tile, axis=1) <= SHORT_BLOCK).astype(jnp.int32)
    cnt_tile = ((cnt_tile + SUBLANES - 1) // SUBLANES) * SUBLANES
    n_e = cnt_tile.sum(axis=0)
    padded = ((n_e + tm - 1) // tm + 1) * tm
    ends = jnp.cumsum(padded)
    starts = ends - padded
    offs = starts[None, :] + jnp.cumsum(cnt_tile, axis=0) - cnt_tile
    fill = starts + n_e
    max_rows = TOP_K * T + nt * N_EXPERTS * (SUBLANES - 1)
    G = -(-max_rows // tm) + 2 * N_EXPERTS + 1
    tile_start = jnp.arange(G, dtype=jnp.int32) * tm
    tile_expert = jnp.minimum(jnp.sum((tile_start[:, None] >= ends[None, :]).astype(jnp.int32), axis=1),
                              N_EXPERTS - 1)
    active = (tile_start < fill[tile_expert]).astype(jnp.int32)
    x_block = jnp.where(active > 0, jnp.arange(G, dtype=jnp.int32), jnp.argmax(active).astype(jnp.int32))
    first = ((active > 0) & (tile_start == starts[tile_expert])).astype(jnp.int32)
    fill = jnp.concatenate([fill, ends[-1:] // tm])
    return (offs.reshape(-1).astype(jnp.int32), short, fill.astype(jnp.int32), tile_expert.astype(jnp.int32),
            x_block, active, first, G * tm)


def kernel(x, attn_norm, ffn_norm, hgrn_w_in, hgrn_lower_bounds, hgrn_out_norm, hgrn_w_out, kv_norm, w_kv,
           dil_w_q, dil_w_out, ffn_w_gate, ffn_w_up, ffn_w_down, moe_w_router, moe_w_gate, moe_w_up,
           moe_w_down, final_norm):
    B, S, D = x.shape
    row = lambda v: v.reshape(1, -1).astype(F32)

    lbs = jnp.cumsum(jax.nn.softmax(hgrn_lower_bounds.astype(F32), axis=0), axis=0)
    h = _hgrn_mixer(x, row(attn_norm[0]), hgrn_w_in[0].astype(BF16), row(lbs[0]), row(hgrn_out_norm[0]),
                    hgrn_w_out[0].astype(BF16))
    h = _dense_ffn(h, row(ffn_norm[0]), ffn_w_gate[0].astype(BF16), ffn_w_up[0].astype(BF16),
                   ffn_w_down[0].astype(BF16))

    cos_k, sin_k = _rope_tables(S, 1.0)
    cos_q, sin_q = _rope_tables(S, ATT_SCALE * LOG2E)
    w_k = _pair_block_columns(w_kv[:, :N_BRANCH * D])
    w_v = w_kv[:, N_BRANCH * D:].astype(BF16)
    w_q = _pair_block_columns(dil_w_q[0])
    ks = _norm_proj(h, row(kv_norm), w_k, cos_k, sin_k, rope=True)
    vs = _norm_proj(h, row(kv_norm), w_v, cos_k, sin_k, rope=False)
    qs = _norm_proj(h, row(attn_norm[1]), w_q, cos_q, sin_q, rope=True)
    os, lses = [], []
    for i, dil in enumerate(DILATIONS):
        o_i, lse_i = _dilated_attn(qs[i], ks[i], vs[i], dil)
        os.append(o_i)
        lses.append(lse_i)

    head_of_col = jnp.arange(D, dtype=jnp.int32) // ATT_HEAD_DIM
    expand = (jnp.arange(LANES, dtype=jnp.int32)[:, None] == head_of_col[None, :]).astype(BF16)
    expand = jnp.concatenate([expand, expand], axis=0)
    wr = jnp.zeros((D, LANES), F32).at[:, :N_EXPERTS].set(moe_w_router[0].astype(F32))
    wr_hi = wr.astype(BF16)
    wr_lo = (wr - wr_hi.astype(F32)).astype(BF16)
    h, hn, route = _attn_combine(os, lses, h, dil_w_out[0].astype(BF16), expand, row(ffn_norm[1]),
                                 jnp.concatenate([wr_hi, wr_lo], axis=1))

    offs, short, fill, tile_expert, x_block, active, first, n_rows = _route_metadata(route, TOKEN_TILE)
    tables = _spread_tables(SHORT_BLOCK) + _spread_tables(TOKEN_TILE)
    xs = _moe_dispatch(route, hn, offs, short, fill, tables, n_rows)
    ys = _moe_experts(xs, tile_expert, x_block, active, first, moe_w_gate[0], moe_w_up[0], moe_w_down[0])
    out = _moe_finish(ys, offs, short, route, h, row(final_norm), tables)
    return out.reshape(B, S, D)
```

```python
import functools

import jax
import jax.numpy as jnp
from jax import lax
from jax.experimental import pallas as pl
from jax.experimental.pallas import tpu as pltpu

F32 = jnp.float32
BF16 = jnp.bfloat16

D_MODEL = 1024
HGRN_HEAD_DIM = 128
HGRN_HEADS = D_MODEL // HGRN_HEAD_DIM
HGRN_CHUNK = 64
ATT_HEAD_DIM = 64
ATT_HEADS = D_MODEL // ATT_HEAD_DIM
DILATIONS = (1, 4, 16)
N_BRANCH = len(DILATIONS)
BAND = 128
ATT_SCALE = ATT_HEAD_DIM ** -0.5
ROPE_THETA = 10000.0
N_EXPERTS = 8
TOP_K = 2
EPS = 1e-6

LANES = 128
SUBLANES = 8
TOKEN_TILE = 256
WIDE_TILE = 512
MOE_ROW_TILE = TOKEN_TILE
VMEM_LIMIT = 56 * 1024 * 1024
NEG_BIG = -1e30
LN2 = 0.6931471805599453
LOG2E = 1.0 / LN2


def _dot(a, b):
    return jnp.dot(a, b, preferred_element_type=F32)


def _dot_nt(a, b):
    return lax.dot_general(a, b, (((1,), (1,)), ((), ())), preferred_element_type=F32)


def _dot_tn(a, b):
    return lax.dot_general(a, b, (((0,), (0,)), ((), ())), preferred_element_type=F32)


def _rms(x, g):
    return x * lax.rsqrt(jnp.mean(x * x, axis=-1, keepdims=True) + EPS) * g


def _sigmoid(x):
    return 1.0 / (1.0 + jnp.exp(-x))


def _silu(x):
    return x * _sigmoid(x)


def _split3(x):
    hi = x.astype(BF16)
    r1 = x - hi.astype(F32)
    mid = r1.astype(BF16)
    lo = (r1 - mid.astype(F32)).astype(BF16)
    return hi, mid, lo


def _const_spec(shape):
    nd = len(shape)
    return pl.BlockSpec(shape, lambda *_: (0,) * nd, pipeline_mode=pl.Buffered(1))


def _params(sem):
    return pltpu.CompilerParams(dimension_semantics=sem, vmem_limit_bytes=VMEM_LIMIT)


def _hgrn_kernel(x_ref, g_ref, win_ref, lb_ref, onorm_ref, wout_ref, out_ref,
                 state_ref, y_ref, o_ref, k_ref, b_ref, oi_ref, s_ref):
    tm = x_ref.shape[1]
    dk = HGRN_HEAD_DIM
    c_len = HGRN_CHUNK

    @pl.when(pl.program_id(1) == 0)
    def _():
        state_ref[...] = jnp.zeros_like(state_ref)

    x = x_ref[0]
    xn = _rms(x, g_ref[...]).astype(BF16)
    y_ref[...] = _dot(xn, win_ref[...])

    r = lax.broadcasted_iota(jnp.int32, (tm, tm), 0)
    c = lax.broadcasted_iota(jnp.int32, (tm, tm), 1)
    tri = jnp.where(c <= r, 1.0, 0.0).astype(BF16)

    lb = lb_ref[...]
    f = lb + (1.0 - lb) * _sigmoid(y_ref[:, D_MODEL:2 * D_MODEL])
    k_ref[...] = 1.0 - f
    lf_hi, lf_mid, lf_lo = _split3(jnp.log(f))
    b_ref[...] = _dot(tri, lf_hi) + _dot(tri, lf_mid) + _dot(tri, lf_lo)

    for h in range(HGRN_HEADS):
        cols = slice(h * dk, (h + 1) * dk)
        q = y_ref[:, h * dk:(h + 1) * dk]
        v16 = y_ref[:, 2 * D_MODEL + h * dk:2 * D_MODEL + (h + 1) * dk].astype(BF16)
        k = k_ref[:, cols]
        b = b_ref[:, cols]
        b_last = b[tm - 1:tm, :]
        st = state_ref[h]
        oi_ref[:, cols] = _dot_nt((q * jnp.exp(b)).astype(BF16), st.astype(BF16))
        k_end = (k * jnp.exp(b_last - b)).astype(BF16)
        state_ref[h] = st * jnp.exp(b_last) + _dot_tn(v16, k_end)
        for ci in range(tm // c_len):
            rows = slice(ci * c_len, (ci + 1) * c_len)
            n_keys = (ci + 1) * c_len
            ref = b[ci * c_len - 1:ci * c_len, :] if ci > 0 else jnp.zeros((1, dk), F32)
            q_dec = (q[rows] * jnp.exp(b[rows] - ref)).astype(BF16)
            k_dec = (k[:n_keys] * jnp.exp(ref - b[:n_keys])).astype(BF16)
            qpos = lax.broadcasted_iota(jnp.int32, (c_len, n_keys), 0) + ci * c_len
            kpos = lax.broadcasted_iota(jnp.int32, (c_len, n_keys), 1)
            s_ref[h, rows, :n_keys] = jnp.where(kpos <= qpos, _dot_nt(q_dec, k_dec), 0.0).astype(BF16)
            if n_keys < tm:
                s_ref[h, rows, n_keys:] = jnp.zeros((c_len, tm - n_keys), BF16)

    for h in range(HGRN_HEADS):
        cols = slice(h * dk, (h + 1) * dk)
        v16 = y_ref[:, 2 * D_MODEL + h * dk:2 * D_MODEL + (h + 1) * dk].astype(BF16)
        gate = y_ref[:, 3 * D_MODEL + h * dk:3 * D_MODEL + (h + 1) * dk]
        o = _dot(s_ref[h], v16) + oi_ref[:, cols]
        o_ref[:, cols] = (_rms(o, onorm_ref[...]) * _silu(gate)).astype(BF16)

    out_ref[0] = x + _dot(o_ref[...], wout_ref[...])


def _hgrn_mixer(x, g, w_in, lb, out_norm, w_out):
    B, S, D = x.shape
    tm = TOKEN_TILE
    return pl.pallas_call(
        _hgrn_kernel,
        grid=(B, S // tm),
        in_specs=[
            pl.BlockSpec((1, tm, D), lambda b, s: (b, s, 0)),
            _const_spec((1, D)),
            _const_spec((D, 4 * D)),
            _const_spec((1, D)),
            _const_spec((1, HGRN_HEAD_DIM)),
            _const_spec((D, D)),
        ],
        out_specs=pl.BlockSpec((1, tm, D), lambda b, s: (b, s, 0)),
        out_shape=jax.ShapeDtypeStruct((B, S, D), F32),
        scratch_shapes=[
            pltpu.VMEM((HGRN_HEADS, HGRN_HEAD_DIM, HGRN_HEAD_DIM), F32),
            pltpu.VMEM((tm, 4 * D), F32),
            pltpu.VMEM((tm, D), BF16),
            pltpu.VMEM((tm, D), F32),
            pltpu.VMEM((tm, D), F32),
            pltpu.VMEM((tm, D), F32),
            pltpu.VMEM((HGRN_HEADS, tm, tm), BF16),
        ],
        compiler_params=_params(("arbitrary", "arbitrary")),
        name="hgrn_mixer",
    )(x, g, w_in, lb, out_norm, w_out)


def _ffn_kernel(x_ref, g_ref, wg_ref, wu_ref, wd_ref, out_ref):
    x = x_ref[0]
    xn = _rms(x, g_ref[...]).astype(BF16)
    a = _silu(_dot(xn, wg_ref[...])) * _dot(xn, wu_ref[...])
    out_ref[0] = x + _dot(a.astype(BF16), wd_ref[...])


def _dense_ffn(x, g, w_gate, w_up, w_down):
    B, S, D = x.shape
    Fd = w_gate.shape[1]
    tm = WIDE_TILE
    return pl.pallas_call(
        _ffn_kernel,
        grid=(B, S // tm),
        in_specs=[
            pl.BlockSpec((1, tm, D), lambda b, s: (b, s, 0)),
            _const_spec((1, D)),
            _const_spec((D, Fd)),
            _const_spec((D, Fd)),
            _const_spec((Fd, D)),
        ],
        out_specs=pl.BlockSpec((1, tm, D), lambda b, s: (b, s, 0)),
        out_shape=jax.ShapeDtypeStruct((B, S, D), F32),
        compiler_params=_params(("arbitrary", "arbitrary")),
        name="dense_ffn",
    )(x, g, w_gate, w_up, w_down)


def _proj_kernel(x_ref, g_ref, w_ref, cos_ref, sin_ref, o0_ref, o1_ref, o2_ref, t_ref, *, rope):
    tm = x_ref.shape[1]
    xn = _rms(x_ref[0], g_ref[...]).astype(BF16)
    out_refs = (o0_ref, o1_ref, o2_ref)
    for j, dil in enumerate(DILATIONS):
        t = _dot(xn, w_ref[:, j * D_MODEL:(j + 1) * D_MODEL])
        for p in range(D_MODEL // LANES):
            cols = slice(p * LANES, (p + 1) * LANES)
            tp = t[:, cols]
            if rope:
                tp = tp * cos_ref[...] + pltpu.roll(tp, LANES // 2, 1) * sin_ref[...]
            if dil == 1:
                out_refs[j][0, 0, :, cols] = tp.astype(BF16)
            else:
                t_ref[p] = tp
                for r in range(dil):
                    out_refs[j][0, r, :, cols] = t_ref[p, pl.ds(r, tm // dil, stride=dil), :].astype(BF16)


def _norm_proj(x, g, w, cos, sin, rope):
    B, S, D = x.shape
    tm = WIDE_TILE
    return pl.pallas_call(
        functools.partial(_proj_kernel, rope=rope),
        grid=(B, S // tm),
        in_specs=[
            pl.BlockSpec((1, tm, D), lambda b, s: (b, s, 0)),
            _const_spec((1, D)),
            _const_spec((D, N_BRANCH * D)),
            pl.BlockSpec((tm, LANES), lambda b, s: (s, 0)),
            pl.BlockSpec((tm, LANES), lambda b, s: (s, 0)),
        ],
        out_specs=[pl.BlockSpec((1, dil, tm // dil, D), lambda b, s: (b, 0, s, 0)) for dil in DILATIONS],
        out_shape=[jax.ShapeDtypeStruct((B, dil, S // dil, D), BF16) for dil in DILATIONS],
        scratch_shapes=[pltpu.VMEM((D // LANES, tm, LANES), F32)],
        compiler_params=_params(("arbitrary", "arbitrary")),
        name="norm_proj_rope" if rope else "norm_proj",
    )(x, g, w, cos, sin)


def _rope_tables(seq_len, scale):
    half = ATT_HEAD_DIM // 2
    inv_freq = ROPE_THETA ** (-jnp.arange(half, dtype=F32) / half)
    ang = jnp.arange(seq_len, dtype=F32)[:, None] * inv_freq[None, :]
    cos = jnp.tile(jnp.cos(ang), (1, LANES // half))
    sin = jnp.tile(jnp.sin(ang), (1, LANES // half))
    sign = jnp.where(jnp.arange(LANES) < LANES // 2, -1.0, 1.0).astype(F32)
    return cos * scale, sin * sign[None, :] * scale


def _pair_block_columns(w):
    rows = w.shape[0]
    half = ATT_HEAD_DIM // 2
    w = w.reshape(rows, -1, 2, 2, half)
    return jnp.swapaxes(w, 2, 3).reshape(rows, -1).astype(BF16)


ATT_BLOCKS_PER_STEP = 2


def _attn_kernel(q_ref, kp_ref, kc_ref, vp_ref, vc_ref, o_ref, lse_ref, s_scr, p_scr, m_scr, l_scr):
    n = pl.program_id(2)
    n_blocks = q_ref.shape[0] // BAND
    qi = lax.broadcasted_iota(jnp.int32, (BAND, 2 * BAND), 0)
    ki = lax.broadcasted_iota(jnp.int32, (BAND, 2 * BAND), 1)
    dist = qi + BAND - ki
    band = (dist >= 0) & (dist <= BAND)
    first_key = jnp.where(n > 0, 0, BAND)
    band = jnp.concatenate([band, band], axis=0)
    kpos = jnp.concatenate([ki, ki], axis=0)
    lane = lax.broadcasted_iota(jnp.int32, (BAND, LANES), 1)
    head_lanes = [(lane // (ATT_HEAD_DIM // 2)) % 2 == a for a in range(2)]
    m_scr[...] = jnp.zeros(m_scr.shape, F32)
    l_scr[...] = jnp.ones(l_scr.shape, F32)
    n_pairs = ATT_HEADS // 2
    units = [(j, p) for j in range(n_blocks) for p in range(n_pairs)]

    def prev_and_cur(prev_ref, cur_ref, j, cols):
        prev = prev_ref[:, cols] if j == 0 else cur_ref[(j - 1) * BAND:j * BAND, cols]
        return jnp.concatenate([prev, cur_ref[j * BAND:(j + 1) * BAND, cols]], axis=0)

    for u, (j, p) in enumerate(units):
        cols = slice(p * LANES, (p + 1) * LANES)
        qp = q_ref[j * BAND:(j + 1) * BAND, cols]
        q2 = jnp.concatenate([jnp.where(head_lanes[a], qp, jnp.zeros_like(qp)) for a in range(2)], axis=0)
        valid = band & (kpos >= first_key) if j == 0 else band
        s_scr[u] = jnp.where(valid, _dot_nt(q2, prev_and_cur(kp_ref, kc_ref, j, cols)), NEG_BIG)
    inv_l = []
    for u, (j, p) in enumerate(units):
        s = s_scr[u]
        m = jnp.max(s, axis=-1, keepdims=True)
        pr = jnp.exp2(s - m)
        l = jnp.sum(pr, axis=-1, keepdims=True)
        p_scr[u] = pr.astype(BF16)
        inv_l.append(1.0 / l)
        for a in range(2):
            rows = slice(a * BAND, (a + 1) * BAND)
            m_scr[j * BAND:(j + 1) * BAND, 2 * p + a:2 * p + a + 1] = m[rows]
            l_scr[j * BAND:(j + 1) * BAND, 2 * p + a:2 * p + a + 1] = l[rows]
    for u, (j, p) in enumerate(units):
        cols = slice(p * LANES, (p + 1) * LANES)
        o = _dot(p_scr[u], prev_and_cur(vp_ref, vc_ref, j, cols)) * inv_l[u]
        o_ref[j * BAND:(j + 1) * BAND, cols] = jnp.where(lane < ATT_HEAD_DIM, o[:BAND], o[BAND:]).astype(o_ref.dtype)
    lse_ref[...] = m_scr[...] * LN2 + jnp.log(l_scr[...])


def _dilated_attn(q, k, v, dil):
    B, _, L, D = q.shape
    qb = min(ATT_BLOCKS_PER_STEP, L // BAND)
    rows = qb * BAND
    n_units = qb * (ATT_HEADS // 2)
    blk = (None, None, rows, D)
    cur = lambda b, r, n: (b, r, n, 0)
    prev = lambda b, r, n: (b, r, jnp.maximum(n * qb - 1, 0), 0)
    return pl.pallas_call(
        _attn_kernel,
        grid=(B, dil, L // rows),
        in_specs=[
            pl.BlockSpec(blk, cur),
            pl.BlockSpec((None, None, BAND, D), prev),
            pl.BlockSpec(blk, cur),
            pl.BlockSpec((None, None, BAND, D), prev),
            pl.BlockSpec(blk, cur),
        ],
        out_specs=[pl.BlockSpec(blk, cur), pl.BlockSpec((None, None, rows, LANES), cur)],
        out_shape=[
            jax.ShapeDtypeStruct((B, dil, L, D), BF16),
            jax.ShapeDtypeStruct((B, dil, L, LANES), F32),
        ],
        scratch_shapes=[
            pltpu.VMEM((n_units, 2 * BAND, 2 * BAND), F32),
            pltpu.VMEM((n_units, 2 * BAND, 2 * BAND), BF16),
            pltpu.VMEM((rows, LANES), F32),
            pltpu.VMEM((rows, LANES), F32),
        ],
        compiler_params=_params(("arbitrary", "arbitrary", "arbitrary")),
        name=f"dilated_attn_d{dil}",
    )(q, k, k, v, v)


def _combine_kernel(o0_ref, o1_ref, o2_ref, l0_ref, l1_ref, l2_ref, h_ref, wout_ref, expand_ref,
                    g_ref, wr_ref, h_out_ref, hn_out_ref, route_ref, o_scr, l_scr):
    tm = h_ref.shape[1]
    o_refs = (o0_ref, o1_ref, o2_ref)
    l_refs = (l0_ref, l1_ref, l2_ref)
    for i, dil in enumerate(DILATIONS):
        for r in range(dil):
            rows = pl.ds(r, tm // dil, stride=dil)
            l_scr[i, rows, :] = l_refs[i][0, r]
            for p in range(D_MODEL // LANES):
                o_scr[i, p, rows, :] = o_refs[i][0, r, :, p * LANES:(p + 1) * LANES].astype(F32)
    lses = [l_scr[i] for i in range(N_BRANCH)]
    m = jnp.maximum(jnp.maximum(lses[0], lses[1]), lses[2])
    es = [jnp.exp(l - m) for l in lses]
    den = es[0] + es[1] + es[2]
    o = None
    for i in range(N_BRANCH):
        w = es[i] / den
        w_hi = w.astype(BF16)
        w_lo = (w - w_hi.astype(F32)).astype(BF16)
        w_full = _dot(jnp.concatenate([w_hi, w_lo], axis=1), expand_ref[...])
        term = w_full * jnp.concatenate([o_scr[i, p] for p in range(D_MODEL // LANES)], axis=1)
        o = term if o is None else o + term
    h = h_ref[0] + _dot(o.astype(BF16), wout_ref[...])
    h_out_ref[...] = h
    hn = _rms(h, g_ref[...])
    hn_hi = hn.astype(BF16)
    hn_out_ref[...] = hn_hi

    hn_lo = (hn - hn_hi.astype(F32)).astype(BF16)
    l_hi = _dot(hn_hi, wr_ref[...])
    logits = l_hi[:, :LANES] + l_hi[:, LANES:] + _dot(hn_lo, wr_ref[:, :LANES])
    lane = lax.broadcasted_iota(jnp.int32, logits.shape, 1)
    lane_f = lane.astype(F32)
    logits = jnp.where(lane < N_EXPERTS, logits, NEG_BIG)
    mx = jnp.max(logits, axis=-1, keepdims=True)
    pe = jnp.exp(logits - mx)
    probs = pe / jnp.sum(pe, axis=-1, keepdims=True)
    probs = jnp.where(lane < N_EXPERTS, probs, -1.0)
    p1 = jnp.max(probs, axis=-1, keepdims=True)
    i1 = jnp.min(jnp.where(probs == p1, lane_f, float(LANES)), axis=-1, keepdims=True)
    rest = jnp.where(lane_f == i1, -1.0, probs)
    p2 = jnp.max(rest, axis=-1, keepdims=True)
    i2 = jnp.min(jnp.where(rest == p2, lane_f, float(LANES)), axis=-1, keepdims=True)
    tot = p1 + p2
    route = jnp.where(lane_f == i1, p1 / tot, 0.0)
    route = jnp.where(lane_f == i2, p2 / tot, route)
    sel = jnp.where((lane_f == i1 + N_EXPERTS) | (lane_f == i2 + N_EXPERTS), 1.0, 0.0)
    route_ref[...] = route + sel


def _attn_combine(os, lses, h, w_out, expand, g, wr):
    B, S, D = h.shape
    T = B * S
    tm = WIDE_TILE
    ts = S // tm
    row = lambda b, s: (b * ts + s, 0)
    o_specs = [pl.BlockSpec((1, dil, tm // dil, D), lambda b, s: (b, 0, s, 0)) for dil in DILATIONS]
    l_specs = [pl.BlockSpec((1, dil, tm // dil, LANES), lambda b, s: (b, 0, s, 0)) for dil in DILATIONS]
    return pl.pallas_call(
        _combine_kernel,
        grid=(B, ts),
        in_specs=o_specs + l_specs + [
            pl.BlockSpec((1, tm, D), lambda b, s: (b, s, 0)),
            _const_spec((D, D)),
            _const_spec((2 * LANES, D)),
            _const_spec((1, D)),
            _const_spec((D, 2 * LANES)),
        ],
        out_specs=[pl.BlockSpec((tm, D), row), pl.BlockSpec((tm, D), row), pl.BlockSpec((tm, LANES), row)],
        out_shape=[
            jax.ShapeDtypeStruct((T, D), F32),
            jax.ShapeDtypeStruct((T, D), BF16),
            jax.ShapeDtypeStruct((T, LANES), F32),
        ],
        scratch_shapes=[pltpu.VMEM((N_BRANCH, D // LANES, tm, LANES), F32),
                        pltpu.VMEM((N_BRANCH, tm, LANES), F32)],
        compiler_params=_params(("arbitrary", "arbitrary")),
        name="attn_combine_route",
    )(*os, *lses, h, w_out, expand, g, wr)


SHORT_BLOCK = 96


def _spread_tables(n_slots):
    n_cols = N_EXPERTS * n_slots
    col_expert = jnp.arange(n_cols, dtype=jnp.int32) // n_slots
    src = jnp.arange(LANES, dtype=jnp.int32)
    spread = (src[:, None] == col_expert[None, :] + N_EXPERTS).astype(BF16)
    slots = (jnp.arange(n_cols, dtype=jnp.int32) % n_slots).astype(F32).reshape(1, n_cols)
    return jnp.concatenate([spread, spread], axis=0), slots


def _slot_one_hot(route, spread_ref, slots_ref, weighted=False):
    tm = route.shape[0]
    lane = lax.broadcasted_iota(jnp.int32, route.shape, 1)
    sel = jnp.where((lane >= N_EXPERTS) & (lane < 2 * N_EXPERTS), route, 0.0)
    r = lax.broadcasted_iota(jnp.int32, (tm, tm), 0)
    c = lax.broadcasted_iota(jnp.int32, (tm, tm), 1)
    strict_lower = jnp.where(c < r, 1.0, 0.0).astype(BF16)
    rank = _dot(strict_lower, sel.astype(BF16))
    key = jnp.where(sel > 0.5, rank, -1.0).astype(BF16)
    hit = _dot(key, spread_ref[:LANES, :]) == slots_ref[...]
    if not weighted:
        return jnp.where(hit, 1.0, 0.0)
    w = pltpu.roll(route, N_EXPERTS, 1)
    w_hi = w.astype(BF16)
    w_lo = (w - w_hi.astype(F32)).astype(BF16)
    return jnp.where(hit, _dot(jnp.concatenate([w_hi, w_lo], axis=1), spread_ref[...]), 0.0)


def _block_copy(src, dst, sem):
    return pltpu.make_async_copy(src, dst, sem)


def _when_short(short_ref, step, fn, tables, tm):
    @pl.when(short_ref[step] > 0)
    def _():
        fn(SHORT_BLOCK, *tables[:2])

    @pl.when(short_ref[step] == 0)
    def _():
        fn(tm, *tables[2:])


def _dispatch_kernel(offs_ref, short_ref, fill_ref, route_ref, hn_ref, sp_s, sl_s, sp_l, sl_l, xs_hbm, stage, sem):
    tables = (sp_s, sl_s, sp_l, sl_l)
    j = pl.program_id(0)
    n = pl.num_programs(0)
    tm = hn_ref.shape[0]
    n_tiles = xs_hbm.shape[0] // tm
    slot = j % 2

    def zero_copy(row):
        return _block_copy(stage.at[1, 0], xs_hbm.at[pl.ds(pl.multiple_of(row, SUBLANES), tm)], sem.at[1])

    @pl.when(j == 0)
    def _():
        stage[1, 0] = jnp.zeros((tm, D_MODEL), F32)
        first_tail = fill_ref[N_EXPERTS]

        def start_tail(t, carry):
            zero_copy(t * tm).start()
            return carry

        def wait_tail(t, carry):
            zero_copy(t * tm).wait()
            return carry

        first = [zero_copy(fill_ref[e]) for e in range(N_EXPERTS)]
        second = [zero_copy(fill_ref[e] + tm) for e in range(N_EXPERTS)]
        for cp in first:
            cp.start()
        lax.fori_loop(first_tail, n_tiles, start_tail, 0)
        for cp in first:
            cp.wait()
        lax.fori_loop(first_tail, n_tiles, wait_tail, 0)
        for cp in second:
            cp.start()
        for cp in second:
            cp.wait()

    def copies(step_slot, step, n_rows):
        return [_block_copy(stage.at[step_slot, e, pl.ds(0, n_rows)],
                            xs_hbm.at[pl.ds(pl.multiple_of(offs_ref[step * N_EXPERTS + e], SUBLANES), n_rows)],
                            sem.at[step_slot])
                for e in range(N_EXPERTS)]

    def place(n_rows, spread_ref, slots_ref):
        one_hot = _slot_one_hot(route_ref[...], spread_ref, slots_ref).astype(BF16)
        placed = _dot_tn(one_hot, hn_ref[...])
        for e in range(N_EXPERTS):
            stage[slot, e, :n_rows] = placed[e * n_rows:(e + 1) * n_rows]

    _when_short(short_ref, j, place, tables, tm)

    @pl.when(j > 0)
    def _():
        _when_short(short_ref, j - 1, lambda n_rows, *_: [cp.wait() for cp in copies(1 - slot, j - 1, n_rows)],
                    tables, tm)

    _when_short(short_ref, j, lambda n_rows, *_: [cp.start() for cp in copies(slot, j, n_rows)], tables, tm)

    @pl.when(j == n - 1)
    def _():
        _when_short(short_ref, j, lambda n_rows, *_: [cp.wait() for cp in copies(slot, j, n_rows)], tables, tm)


def _table_specs(tm):
    return [_const_spec((2 * LANES, N_EXPERTS * SHORT_BLOCK)), _const_spec((1, N_EXPERTS * SHORT_BLOCK)),
            _const_spec((2 * LANES, N_EXPERTS * tm)), _const_spec((1, N_EXPERTS * tm))]


def _moe_dispatch(route, hn, offs, short, fill, tables, n_rows):
    T, D = hn.shape
    tm = TOKEN_TILE
    grid_spec = pltpu.PrefetchScalarGridSpec(
        num_scalar_prefetch=3,
        grid=(T // tm,),
        in_specs=[
            pl.BlockSpec((tm, LANES), lambda j, *_: (j, 0)),
            pl.BlockSpec((tm, D), lambda j, *_: (j, 0)),
        ] + _table_specs(tm),
        out_specs=pl.BlockSpec(memory_space=pl.ANY),
        scratch_shapes=[pltpu.VMEM((2, N_EXPERTS, tm, D), F32), pltpu.SemaphoreType.DMA((2,))],
    )
    return pl.pallas_call(
        _dispatch_kernel,
        grid_spec=grid_spec,
        out_shape=jax.ShapeDtypeStruct((n_rows, D), F32),
        compiler_params=_params(("arbitrary",)),
        name="moe_dispatch",
    )(offs, short, fill, route, hn, *tables)


EXPERT_F_CHUNK = 512


def _expert_kernel(te_ref, xb_ref, act_ref, first_ref, x_ref, wg_hbm, wu_hbm, wd_hbm, y_ref,
                   wg_buf, wu_buf, wd_buf, stg_gu, stg_d, sem):
    del xb_ref
    i = pl.program_id(0)
    tm = x_ref.shape[0]
    fc = EXPERT_F_CHUNK
    n_chunks = wg_buf.shape[1] // fc

    def chunk_copies(e, c, slot):
        cols = pl.ds(c * fc, fc)
        return [pltpu.make_async_copy(wg_hbm.at[e, :, cols], stg_gu.at[slot, 0], sem.at[slot]),
                pltpu.make_async_copy(wu_hbm.at[e, :, cols], stg_gu.at[slot, 1], sem.at[slot]),
                pltpu.make_async_copy(wd_hbm.at[e, cols, :], stg_d.at[slot], sem.at[slot])]

    def convert_chunk(c, slot):
        cs = slice(c * fc, (c + 1) * fc)
        wg_buf[:, cs] = stg_gu[slot, 0].astype(BF16)
        wu_buf[:, cs] = stg_gu[slot, 1].astype(BF16)
        wd_buf[cs, :] = stg_d[slot].astype(BF16)

    def swiglu_tile(load):
        x = x_ref[...].astype(BF16)
        acc = jnp.zeros((tm, D_MODEL), F32)
        if load is not None:
            for cp in chunk_copies(load, 0, 0):
                cp.start()
        for c in range(n_chunks):
            if load is not None:
                if c + 1 < n_chunks:
                    for cp in chunk_copies(load, c + 1, (c + 1) % 2):
                        cp.start()
                for cp in chunk_copies(load, c, c % 2):
                    cp.wait()
                convert_chunk(c, c % 2)
            cs = slice(c * fc, (c + 1) * fc)
            a = _silu(_dot(x, wg_buf[:, cs])) * _dot(x, wu_buf[:, cs])
            acc = acc + _dot(a.astype(BF16), wd_buf[cs, :])
        y_ref[...] = acc

    @pl.when((act_ref[i] > 0) & (first_ref[i] > 0))
    def _():
        swiglu_tile(te_ref[i])

    @pl.when((act_ref[i] > 0) & (first_ref[i] == 0))
    def _():
        swiglu_tile(None)

    @pl.when(act_ref[i] == 0)
    def _():
        y_ref[...] = jnp.zeros_like(y_ref)


def _moe_experts(xs, tile_expert, x_block, active, first, w_gate, w_up, w_down):
    n_rows, D = xs.shape
    Fd = w_gate.shape[2]
    tm = MOE_ROW_TILE
    fc = EXPERT_F_CHUNK
    G = n_rows // tm
    grid_spec = pltpu.PrefetchScalarGridSpec(
        num_scalar_prefetch=4,
        grid=(G,),
        in_specs=[
            pl.BlockSpec((tm, D), lambda i, te, xb, *_: (xb[i], 0)),
            pl.BlockSpec(memory_space=pl.ANY),
            pl.BlockSpec(memory_space=pl.ANY),
            pl.BlockSpec(memory_space=pl.ANY),
        ],
        out_specs=pl.BlockSpec((tm, D), lambda i, *_: (i, 0)),
        scratch_shapes=[
            pltpu.VMEM((D, Fd), BF16),
            pltpu.VMEM((D, Fd), BF16),
            pltpu.VMEM((Fd, D), BF16),
            pltpu.VMEM((2, 2, D, fc), F32),
            pltpu.VMEM((2, fc, D), F32),
            pltpu.SemaphoreType.DMA((2,)),
        ],
    )
    return pl.pallas_call(
        _expert_kernel,
        grid_spec=grid_spec,
        out_shape=jax.ShapeDtypeStruct((n_rows, D), F32),
        compiler_params=_params(("arbitrary",)),
        name="moe_experts",
    )(tile_expert, x_block, active, first, xs, w_gate, w_up, w_down)


def _finish_kernel(offs_ref, short_ref, route_ref, h_ref, g_ref, sp_s, sl_s, sp_l, sl_l, ys_hbm, out_ref, ybuf, sem):
    tables = (sp_s, sl_s, sp_l, sl_l)
    j = pl.program_id(0)
    n = pl.num_programs(0)
    tm = h_ref.shape[0]
    slot = j % 2

    def copies(step_slot, step, n_rows):
        return [_block_copy(ys_hbm.at[pl.ds(pl.multiple_of(offs_ref[step * N_EXPERTS + e], SUBLANES), n_rows)],
                            ybuf.at[step_slot, e, pl.ds(0, n_rows)], sem.at[step_slot])
                for e in range(N_EXPERTS)]

    @pl.when(j == 0)
    def _():
        _when_short(short_ref, 0, lambda n_rows, *_: [cp.start() for cp in copies(0, 0, n_rows)], tables, tm)

    @pl.when(j + 1 < n)
    def _():
        _when_short(short_ref, j + 1, lambda n_rows, *_: [cp.start() for cp in copies(1 - slot, j + 1, n_rows)],
                    tables, tm)

    def gather(n_rows, spread_ref, slots_ref):
        w = _slot_one_hot(route_ref[...], spread_ref, slots_ref, weighted=True)
        w_hi = w.astype(BF16)
        w_lo = (w - w_hi.astype(F32)).astype(BF16)
        for cp in copies(slot, j, n_rows):
            cp.wait()
        y = jnp.concatenate([ybuf[slot, e, :n_rows].astype(BF16) for e in range(N_EXPERTS)], axis=0)
        out_ref[...] = _rms(h_ref[...] + _dot(w_hi, y) + _dot(w_lo, y), g_ref[...])

    _when_short(short_ref, j, gather, tables, tm)


def _moe_finish(ys, offs, short, route, h, g, tables):
    T, D = h.shape
    tm = TOKEN_TILE
    grid_spec = pltpu.PrefetchScalarGridSpec(
        num_scalar_prefetch=2,
        grid=(T // tm,),
        in_specs=[
            pl.BlockSpec((tm, LANES), lambda j, *_: (j, 0)),
            pl.BlockSpec((tm, D), lambda j, *_: (j, 0)),
            pl.BlockSpec((1, D), lambda j, *_: (0, 0)),
        ] + _table_specs(tm) + [
            pl.BlockSpec(memory_space=pl.ANY),
        ],
        out_specs=pl.BlockSpec((tm, D), lambda j, *_: (j, 0)),
        scratch_shapes=[pltpu.VMEM((2, N_EXPERTS, tm, D), F32), pltpu.SemaphoreType.DMA((2,))],
    )
    return pl.pallas_call(
        _finish_kernel,
        grid_spec=grid_spec,
        out_shape=jax.ShapeDtypeStruct((T, D), F32),
        compiler_params=_params(("arbitrary",)),
        name="moe_finish",
    )(offs, short, route, h, g, *tables, ys)


def _route_metadata(route, tm):
    T = route.shape[0]
    nt = T // tm
    sel = (route[:, N_EXPERTS:2 * N_EXPERTS] > 0.5).astype(jnp.int32)
    cnt_tile = sel.reshape(nt, tm, N_EXPERTS).sum(axis=1)
    short = (jnp.max(cnt_tile, axis=1) <= SHORT_BLOCK).astype(jnp.int32)
    cnt_tile = ((cnt_tile + SUBLANES - 1) // SUBLANES) * SUBLANES
    n_e = cnt_tile.sum(axis=0)
    padded = ((n_e + tm - 1) // tm + 1) * tm
    ends = jnp.cumsum(padded)
    starts = ends - padded
    offs = starts[None, :] + jnp.cumsum(cnt_tile, axis=0) - cnt_tile
    fill = starts + n_e
    max_rows = TOP_K * T + nt * N_EXPERTS * (SUBLANES - 1)
    G = -(-max_rows // tm) + 2 * N_EXPERTS + 1
    tile_start = jnp.arange(G, dtype=jnp.int32) * tm
    tile_expert = jnp.minimum(jnp.sum((tile_start[:, None] >= ends[None, :]).astype(jnp.int32), axis=1),
                              N_EXPERTS - 1)
    active = (tile_start < fill[tile_expert]).astype(jnp.int32)
    x_block = jnp.where(active > 0, jnp.arange(G, dtype=jnp.int32), jnp.argmax(active).astype(jnp.int32))
    first = ((active > 0) & (tile_start == starts[tile_expert])).astype(jnp.int32)
    fill = jnp.concatenate([fill, ends[-1:] // tm])
    return (offs.reshape(-1).astype(jnp.int32), short, fill.astype(jnp.int32), tile_expert.astype(jnp.int32),
            x_block, active, first, G * tm)


def kernel(x, attn_norm, ffn_norm, hgrn_w_in, hgrn_lower_bounds, hgrn_out_norm, hgrn_w_out, kv_norm, w_kv,
           dil_w_q, dil_w_out, ffn_w_gate, ffn_w_up, ffn_w_down, moe_w_router, moe_w_gate, moe_w_up,
           moe_w_down, final_norm):
    B, S, D = x.shape
    row = lambda v: v.reshape(1, -1).astype(F32)

    lbs = jnp.cumsum(jax.nn.softmax(hgrn_lower_bounds.astype(F32), axis=0), axis=0)
    h = _hgrn_mixer(x, row(attn_norm[0]), hgrn_w_in[0].astype(BF16), row(lbs[0]), row(hgrn_out_norm[0]),
                    hgrn_w_out[0].astype(BF16))
    h = _dense_ffn(h, row(ffn_norm[0]), ffn_w_gate[0].astype(BF16), ffn_w_up[0].astype(BF16),
                   ffn_w_down[0].astype(BF16))

    cos_k, sin_k = _rope_tables(S, 1.0)
    cos_q, sin_q = _rope_tables(S, ATT_SCALE * LOG2E)
    w_k = _pair_block_columns(w_kv[:, :N_BRANCH * D])
    w_v = w_kv[:, N_BRANCH * D:].astype(BF16)
    w_q = _pair_block_columns(dil_w_q[0])
    ks = _norm_proj(h, row(kv_norm), w_k, cos_k, sin_k, rope=True)
    vs = _norm_proj(h, row(kv_norm), w_v, cos_k, sin_k, rope=False)
    qs = _norm_proj(h, row(attn_norm[1]), w_q, cos_q, sin_q, rope=True)
    os, lses = [], []
    for i, dil in enumerate(DILATIONS):
        o_i, lse_i = _dilated_attn(qs[i], ks[i], vs[i], dil)
        os.append(o_i)
        lses.append(lse_i)

    head_of_col = jnp.arange(D, dtype=jnp.int32) // ATT_HEAD_DIM
    expand = (jnp.arange(LANES, dtype=jnp.int32)[:, None] == head_of_col[None, :]).astype(BF16)
    expand = jnp.concatenate([expand, expand], axis=0)
    wr = jnp.zeros((D, LANES), F32).at[:, :N_EXPERTS].set(moe_w_router[0].astype(F32))
    wr_hi = wr.astype(BF16)
    wr_lo = (wr - wr_hi.astype(F32)).astype(BF16)
    h, hn, route = _attn_combine(os, lses, h, dil_w_out[0].astype(BF16), expand, row(ffn_norm[1]),
                                 jnp.concatenate([wr_hi, wr_lo], axis=1))

    offs, short, fill, tile_expert, x_block, active, first, n_rows = _route_metadata(route, TOKEN_TILE)
    tables = _spread_tables(SHORT_BLOCK) + _spread_tables(TOKEN_TILE)
    xs = _moe_dispatch(route, hn, offs, short, fill, tables, n_rows)
    ys = _moe_experts(xs, tile_expert, x_block, active, first, moe_w_gate[0], moe_w_up[0], moe_w_down[0])
    out = _moe_finish(ys, offs, short, route, h, row(final_norm), tables)
    return out.reshape(B, S, D)
```

```python
import jax
import jax.numpy as jnp
from jax import lax
from jax.experimental import pallas as pl
from jax.experimental.pallas import tpu as pltpu

F32 = jnp.float32
BF16 = jnp.bfloat16

D_MODEL = 1024
HGRN_HEAD_DIM = 128
HGRN_HEADS = D_MODEL // HGRN_HEAD_DIM
HGRN_CHUNK = 64
ATT_HEAD_DIM = 64
ATT_HEADS = D_MODEL // ATT_HEAD_DIM
DILATIONS = (1, 4, 16)
N_BRANCH = len(DILATIONS)
BAND = 128
ATT_SCALE = ATT_HEAD_DIM ** -0.5
ROPE_THETA = 10000.0
N_EXPERTS = 8
TOP_K = 2
EPS = 1e-6

LANES = 128
SUBLANES = 8
TOKEN_TILE = 256
WIDE_TILE = 512
MOE_ROW_TILE = TOKEN_TILE
VMEM_LIMIT = 56 * 1024 * 1024
NEG_BIG = -1e30
LN2 = 0.6931471805599453
LOG2E = 1.0 / LN2


def _dot(a, b):
    return jnp.dot(a, b, preferred_element_type=F32)


def _dot_nt(a, b):
    return lax.dot_general(a, b, (((1,), (1,)), ((), ())), preferred_element_type=F32)


def _dot_tn(a, b):
    return lax.dot_general(a, b, (((0,), (0,)), ((), ())), preferred_element_type=F32)


def _rms(x, g):
    return x * lax.rsqrt(jnp.mean(x * x, axis=-1, keepdims=True) + EPS) * g


def _sigmoid(x):
    return 1.0 / (1.0 + jnp.exp(-x))


def _silu(x):
    return x * _sigmoid(x)


def _split3(x):
    hi = x.astype(BF16)
    r1 = x - hi.astype(F32)
    mid = r1.astype(BF16)
    lo = (r1 - mid.astype(F32)).astype(BF16)
    return hi, mid, lo


def _const_spec(shape):
    nd = len(shape)
    return pl.BlockSpec(shape, lambda *_: (0,) * nd, pipeline_mode=pl.Buffered(1))


def _params(sem):
    return pltpu.CompilerParams(dimension_semantics=sem, vmem_limit_bytes=VMEM_LIMIT)


def _hgrn_kernel(x_ref, g_ref, win_ref, lb_ref, onorm_ref, wout_ref, out_ref,
                 state_ref, y_ref, o_ref, k_ref, b_ref, oi_ref, s_ref):
    tm = x_ref.shape[1]
    dk = HGRN_HEAD_DIM
    c_len = HGRN_CHUNK

    @pl.when(pl.program_id(1) == 0)
    def _():
        state_ref[...] = jnp.zeros_like(state_ref)

    x = x_ref[0]
    xn = _rms(x, g_ref[...]).astype(BF16)
    y_ref[...] = _dot(xn, win_ref[...])

    r = lax.broadcasted_iota(jnp.int32, (tm, tm), 0)
    c = lax.broadcasted_iota(jnp.int32, (tm, tm), 1)
    tri = jnp.where(c <= r, 1.0, 0.0).astype(BF16)

    lb = lb_ref[...]
    f = lb + (1.0 - lb) * _sigmoid(y_ref[:, D_MODEL:2 * D_MODEL])
    k_ref[...] = 1.0 - f
    lf_hi, lf_mid, lf_lo = _split3(jnp.log(f))
    b_ref[...] = _dot(tri, lf_hi) + _dot(tri, lf_mid) + _dot(tri, lf_lo)

    for h in range(HGRN_HEADS):
        cols = slice(h * dk, (h + 1) * dk)
        q = y_ref[:, h * dk:(h + 1) * dk]
        v16 = y_ref[:, 2 * D_MODEL + h * dk:2 * D_MODEL + (h + 1) * dk].astype(BF16)
        k = k_ref[:, cols]
        b = b_ref[:, cols]
        b_last = b[tm - 1:tm, :]
        st = state_ref[h]
        oi_ref[:, cols] = _dot_nt((q * jnp.exp(b)).astype(BF16), st.astype(BF16))
        k_end = (k * jnp.exp(b_last - b)).astype(BF16)
        state_ref[h] = st * jnp.exp(b_last) + _dot_tn(v16, k_end)
        for ci in range(tm // c_len):
            rows = slice(ci * c_len, (ci + 1) * c_len)
            n_keys = (ci + 1) * c_len
            ref = b[ci * c_len - 1:ci * c_len, :] if ci > 0 else jnp.zeros((1, dk), F32)
            q_dec = (q[rows] * jnp.exp(b[rows] - ref)).astype(BF16)
            k_dec = (k[:n_keys] * jnp.exp(ref - b[:n_keys])).astype(BF16)
            qpos = lax.broadcasted_iota(jnp.int32, (c_len, n_keys), 0) + ci * c_len
            kpos = lax.broadcasted_iota(jnp.int32, (c_len, n_keys), 1)
            s_ref[h, rows, :n_keys] = jnp.where(kpos <= qpos, _dot_nt(q_dec, k_dec), 0.0).astype(BF16)
            if n_keys < tm:
                s_ref[h, rows, n_keys:] = jnp.zeros((c_len, tm - n_keys), BF16)

    for h in range(HGRN_HEADS):
        cols = slice(h * dk, (h + 1) * dk)
        v16 = y_ref[:, 2 * D_MODEL + h * dk:2 * D_MODEL + (h + 1) * dk].astype(BF16)
        gate = y_ref[:, 3 * D_MODEL + h * dk:3 * D_MODEL + (h + 1) * dk]
        o = _dot(s_ref[h], v16) + oi_ref[:, cols]
        o_ref[:, cols] = (_rms(o, onorm_ref[...]) * _silu(gate)).astype(BF16)

    out_ref[0] = x + _dot(o_ref[...], wout_ref[...])


def _hgrn_mixer(x, g, w_in, lb, out_norm, w_out):
    B, S, D = x.shape
    tm = TOKEN_TILE
    return pl.pallas_call(
        _hgrn_kernel,
        grid=(B, S // tm),
        in_specs=[
            pl.BlockSpec((1, tm, D), lambda b, s: (b, s, 0)),
            _const_spec((1, D)),
            _const_spec((D, 4 * D)),
            _const_spec((1, D)),
            _const_spec((1, HGRN_HEAD_DIM)),
            _const_spec((D, D)),
        ],
        out_specs=pl.BlockSpec((1, tm, D), lambda b, s: (b, s, 0)),
        out_shape=jax.ShapeDtypeStruct((B, S, D), F32),
        scratch_shapes=[
            pltpu.VMEM((HGRN_HEADS, HGRN_HEAD_DIM, HGRN_HEAD_DIM), F32),
            pltpu.VMEM((tm, 4 * D), F32),
            pltpu.VMEM((tm, D), BF16),
            pltpu.VMEM((tm, D), F32),
            pltpu.VMEM((tm, D), F32),
            pltpu.VMEM((tm, D), F32),
            pltpu.VMEM((HGRN_HEADS, tm, tm), BF16),
        ],
        compiler_params=_params(("arbitrary", "arbitrary")),
        name="hgrn_mixer",
    )(x, g, w_in, lb, out_norm, w_out)


def _ffn_kernel(x_ref, g_ref, wg_ref, wu_ref, wd_ref, out_ref):
    x = x_ref[0]
    xn = _rms(x, g_ref[...]).astype(BF16)
    a = _silu(_dot(xn, wg_ref[...])) * _dot(xn, wu_ref[...])
    out_ref[0] = x + _dot(a.astype(BF16), wd_ref[...])


def _dense_ffn(x, g, w_gate, w_up, w_down):
    B, S, D = x.shape
    Fd = w_gate.shape[1]
    tm = WIDE_TILE
    return pl.pallas_call(
        _ffn_kernel,
        grid=(B, S // tm),
        in_specs=[
            pl.BlockSpec((1, tm, D), lambda b, s: (b, s, 0)),
            _const_spec((1, D)),
            _const_spec((D, Fd)),
            _const_spec((D, Fd)),
            _const_spec((Fd, D)),
        ],
        out_specs=pl.BlockSpec((1, tm, D), lambda b, s: (b, s, 0)),
        out_shape=jax.ShapeDtypeStruct((B, S, D), F32),
        compiler_params=_params(("arbitrary", "arbitrary")),
        name="dense_ffn",
    )(x, g, w_gate, w_up, w_down)


def _proj_kernel(x_ref, gkv_ref, gq_ref, wk_ref, wv_ref, wq_ref, cosk_ref, sink_ref, cosq_ref, sinq_ref,
                 k0_ref, k1_ref, k2_ref, v0_ref, v1_ref, v2_ref, q0_ref, q1_ref, q2_ref, t_ref):
    tm = x_ref.shape[1]
    x = x_ref[0]
    unit = x * lax.rsqrt(jnp.mean(x * x, axis=-1, keepdims=True) + EPS)
    xn_kv = (unit * gkv_ref[...]).astype(BF16)
    xn_q = (unit * gq_ref[...]).astype(BF16)
    groups = [
        (xn_kv, wk_ref, (cosk_ref, sink_ref), (k0_ref, k1_ref, k2_ref)),
        (xn_kv, wv_ref, None, (v0_ref, v1_ref, v2_ref)),
        (xn_q, wq_ref, (cosq_ref, sinq_ref), (q0_ref, q1_ref, q2_ref)),
    ]
    for xn, w_ref, tables, out_refs in groups:
        for j, dil in enumerate(DILATIONS):
            t = _dot(xn, w_ref[:, j * D_MODEL:(j + 1) * D_MODEL])
            for p in range(D_MODEL // LANES):
                cols = slice(p * LANES, (p + 1) * LANES)
                tp = t[:, cols]
                if tables is not None:
                    tp = tp * tables[0][...] + pltpu.roll(tp, LANES // 2, 1) * tables[1][...]
                if dil == 1:
                    out_refs[j][0, 0, :, cols] = tp.astype(BF16)
                else:
                    t_ref[p] = tp
                    for r in range(dil):
                        out_refs[j][0, r, :, cols] = t_ref[p, pl.ds(r, tm // dil, stride=dil), :].astype(BF16)


def _norm_proj(x, g_kv, g_q, w_k, w_v, w_q, tables_k, tables_q):
    B, S, D = x.shape
    tm = WIDE_TILE
    table_spec = pl.BlockSpec((tm, LANES), lambda b, s: (s, 0))
    outs = pl.pallas_call(
        _proj_kernel,
        grid=(B, S // tm),
        in_specs=[pl.BlockSpec((1, tm, D), lambda b, s: (b, s, 0)), _const_spec((1, D)), _const_spec((1, D))]
        + [_const_spec((D, N_BRANCH * D))] * 3 + [table_spec] * 4,
        out_specs=[pl.BlockSpec((1, dil, tm // dil, D), lambda b, s: (b, 0, s, 0)) for dil in DILATIONS] * 3,
        out_shape=[jax.ShapeDtypeStruct((B, dil, S // dil, D), BF16) for dil in DILATIONS] * 3,
        scratch_shapes=[pltpu.VMEM((D // LANES, tm, LANES), F32)],
        compiler_params=_params(("arbitrary", "arbitrary")),
        name="norm_proj_kvq",
    )(x, g_kv, g_q, w_k, w_v, w_q, *tables_k, *tables_q)
    return outs[0:3], outs[3:6], outs[6:9]


def _rope_tables(seq_len, scale):
    half = ATT_HEAD_DIM // 2
    inv_freq = ROPE_THETA ** (-jnp.arange(half, dtype=F32) / half)
    ang = jnp.arange(seq_len, dtype=F32)[:, None] * inv_freq[None, :]
    cos = jnp.tile(jnp.cos(ang), (1, LANES // half))
    sin = jnp.tile(jnp.sin(ang), (1, LANES // half))
    sign = jnp.where(jnp.arange(LANES) < LANES // 2, -1.0, 1.0).astype(F32)
    return cos * scale, sin * sign[None, :] * scale


def _pair_block_columns(w):
    rows = w.shape[0]
    half = ATT_HEAD_DIM // 2
    w = w.reshape(rows, -1, 2, 2, half)
    return jnp.swapaxes(w, 2, 3).reshape(rows, -1).astype(BF16)


ATT_BLOCKS_PER_STEP = 2


def _attn_kernel(q_ref, kp_ref, kc_ref, vp_ref, vc_ref, o_ref, lse_ref, s_scr, p_scr, m_scr, l_scr):
    n = pl.program_id(2)
    n_blocks = q_ref.shape[0] // BAND
    qi = lax.broadcasted_iota(jnp.int32, (BAND, 2 * BAND), 0)
    ki = lax.broadcasted_iota(jnp.int32, (BAND, 2 * BAND), 1)
    dist = qi + BAND - ki
    band = (dist >= 0) & (dist <= BAND)
    first_key = jnp.where(n > 0, 0, BAND)
    band = jnp.concatenate([band, band], axis=0)
    kpos = jnp.concatenate([ki, ki], axis=0)
    lane = lax.broadcasted_iota(jnp.int32, (BAND, LANES), 1)
    head_lanes = [(lane // (ATT_HEAD_DIM // 2)) % 2 == a for a in range(2)]
    m_scr[...] = jnp.zeros(m_scr.shape, F32)
    l_scr[...] = jnp.ones(l_scr.shape, F32)
    n_pairs = ATT_HEADS // 2
    units = [(j, p) for j in range(n_blocks) for p in range(n_pairs)]

    def prev_and_cur(prev_ref, cur_ref, j, cols):
        prev = prev_ref[:, cols] if j == 0 else cur_ref[(j - 1) * BAND:j * BAND, cols]
        return jnp.concatenate([prev, cur_ref[j * BAND:(j + 1) * BAND, cols]], axis=0)

    for u, (j, p) in enumerate(units):
        cols = slice(p * LANES, (p + 1) * LANES)
        qp = q_ref[j * BAND:(j + 1) * BAND, cols]
        q2 = jnp.concatenate([jnp.where(head_lanes[a], qp, jnp.zeros_like(qp)) for a in range(2)], axis=0)
        valid = band & (kpos >= first_key) if j == 0 else band
        s_scr[u] = jnp.where(valid, _dot_nt(q2, prev_and_cur(kp_ref, kc_ref, j, cols)), NEG_BIG)
    inv_l = []
    for u, (j, p) in enumerate(units):
        s = s_scr[u]
        m = jnp.max(s, axis=-1, keepdims=True)
        pr = jnp.exp2(s - m)
        l = jnp.sum(pr, axis=-1, keepdims=True)
        p_scr[u] = pr.astype(BF16)
        inv_l.append(1.0 / l)
        for a in range(2):
            rows = slice(a * BAND, (a + 1) * BAND)
            m_scr[j * BAND:(j + 1) * BAND, 2 * p + a:2 * p + a + 1] = m[rows]
            l_scr[j * BAND:(j + 1) * BAND, 2 * p + a:2 * p + a + 1] = l[rows]
    for u, (j, p) in enumerate(units):
        cols = slice(p * LANES, (p + 1) * LANES)
        o = _dot(p_scr[u], prev_and_cur(vp_ref, vc_ref, j, cols)) * inv_l[u]
        o_ref[j * BAND:(j + 1) * BAND, cols] = jnp.where(lane < ATT_HEAD_DIM, o[:BAND], o[BAND:]).astype(o_ref.dtype)
    lse_ref[...] = m_scr[...] * LN2 + jnp.log(l_scr[...])


def _dilated_attn(q, k, v, dil):
    B, _, L, D = q.shape
    qb = min(ATT_BLOCKS_PER_STEP, L // BAND)
    rows = qb * BAND
    n_units = qb * (ATT_HEADS // 2)
    blk = (None, None, rows, D)
    cur = lambda b, r, n: (b, r, n, 0)
    prev = lambda b, r, n: (b, r, jnp.maximum(n * qb - 1, 0), 0)
    return pl.pallas_call(
        _attn_kernel,
        grid=(B, dil, L // rows),
        in_specs=[
            pl.BlockSpec(blk, cur),
            pl.BlockSpec((None, None, BAND, D), prev),
            pl.BlockSpec(blk, cur),
            pl.BlockSpec((None, None, BAND, D), prev),
            pl.BlockSpec(blk, cur),
        ],
        out_specs=[pl.BlockSpec(blk, cur), pl.BlockSpec((None, None, rows, LANES), cur)],
        out_shape=[
            jax.ShapeDtypeStruct((B, dil, L, D), BF16),
            jax.ShapeDtypeStruct((B, dil, L, LANES), F32),
        ],
        scratch_shapes=[
            pltpu.VMEM((n_units, 2 * BAND, 2 * BAND), F32),
            pltpu.VMEM((n_units, 2 * BAND, 2 * BAND), BF16),
            pltpu.VMEM((rows, LANES), F32),
            pltpu.VMEM((rows, LANES), F32),
        ],
        compiler_params=_params(("arbitrary", "arbitrary", "arbitrary")),
        name=f"dilated_attn_d{dil}",
    )(q, k, k, v, v)


def _combine_kernel(o0_ref, o1_ref, o2_ref, l0_ref, l1_ref, l2_ref, h_ref, wout_ref, expand_ref,
                    g_ref, wr_ref, h_out_ref, hn_out_ref, route_ref, o_scr, l_scr):
    tm = h_ref.shape[1]
    o_refs = (o0_ref, o1_ref, o2_ref)
    l_refs = (l0_ref, l1_ref, l2_ref)
    for i, dil in enumerate(DILATIONS):
        for r in range(dil):
            rows = pl.ds(r, tm // dil, stride=dil)
            l_scr[i, rows, :] = l_refs[i][0, r]
            for p in range(D_MODEL // LANES):
                o_scr[i, p, rows, :] = o_refs[i][0, r, :, p * LANES:(p + 1) * LANES].astype(F32)
    lses = [l_scr[i] for i in range(N_BRANCH)]
    m = jnp.maximum(jnp.maximum(lses[0], lses[1]), lses[2])
    es = [jnp.exp(l - m) for l in lses]
    den = es[0] + es[1] + es[2]
    o = None
    for i in range(N_BRANCH):
        w = es[i] / den
        w_hi = w.astype(BF16)
        w_lo = (w - w_hi.astype(F32)).astype(BF16)
        w_full = _dot(jnp.concatenate([w_hi, w_lo], axis=1), expand_ref[...])
        term = w_full * jnp.concatenate([o_scr[i, p] for p in range(D_MODEL // LANES)], axis=1)
        o = term if o is None else o + term
    h = h_ref[0] + _dot(o.astype(BF16), wout_ref[...])
    h_out_ref[...] = h
    hn = _rms(h, g_ref[...])
    hn_hi = hn.astype(BF16)
    hn_out_ref[...] = hn_hi

    hn_lo = (hn - hn_hi.astype(F32)).astype(BF16)
    l_hi = _dot(hn_hi, wr_ref[...])
    logits = l_hi[:, :LANES] + l_hi[:, LANES:] + _dot(hn_lo, wr_ref[:, :LANES])
    lane = lax.broadcasted_iota(jnp.int32, logits.shape, 1)
    lane_f = lane.astype(F32)
    logits = jnp.where(lane < N_EXPERTS, logits, NEG_BIG)
    mx = jnp.max(logits, axis=-1, keepdims=True)
    pe = jnp.exp(logits - mx)
    probs = pe / jnp.sum(pe, axis=-1, keepdims=True)
    probs = jnp.where(lane < N_EXPERTS, probs, -1.0)
    p1 = jnp.max(probs, axis=-1, keepdims=True)
    i1 = jnp.min(jnp.where(probs == p1, lane_f, float(LANES)), axis=-1, keepdims=True)
    rest = jnp.where(lane_f == i1, -1.0, probs)
    p2 = jnp.max(rest, axis=-1, keepdims=True)
    i2 = jnp.min(jnp.where(rest == p2, lane_f, float(LANES)), axis=-1, keepdims=True)
    tot = p1 + p2
    route = jnp.where(lane_f == i1, p1 / tot, 0.0)
    route = jnp.where(lane_f == i2, p2 / tot, route)
    sel = jnp.where((lane_f == i1 + N_EXPERTS) | (lane_f == i2 + N_EXPERTS), 1.0, 0.0)
    route_ref[...] = route + sel


def _attn_combine(os, lses, h, w_out, expand, g, wr):
    B, S, D = h.shape
    T = B * S
    tm = WIDE_TILE
    ts = S // tm
    row = lambda b, s: (b * ts + s, 0)
    o_specs = [pl.BlockSpec((1, dil, tm // dil, D), lambda b, s: (b, 0, s, 0)) for dil in DILATIONS]
    l_specs = [pl.BlockSpec((1, dil, tm // dil, LANES), lambda b, s: (b, 0, s, 0)) for dil in DILATIONS]
    return pl.pallas_call(
        _combine_kernel,
        grid=(B, ts),
        in_specs=o_specs + l_specs + [
            pl.BlockSpec((1, tm, D), lambda b, s: (b, s, 0)),
            _const_spec((D, D)),
            _const_spec((2 * LANES, D)),
            _const_spec((1, D)),
            _const_spec((D, 2 * LANES)),
        ],
        out_specs=[pl.BlockSpec((tm, D), row), pl.BlockSpec((tm, D), row), pl.BlockSpec((tm, LANES), row)],
        out_shape=[
            jax.ShapeDtypeStruct((T, D), F32),
            jax.ShapeDtypeStruct((T, D), BF16),
            jax.ShapeDtypeStruct((T, LANES), F32),
        ],
        scratch_shapes=[pltpu.VMEM((N_BRANCH, D // LANES, tm, LANES), F32),
                        pltpu.VMEM((N_BRANCH, tm, LANES), F32)],
        compiler_params=_params(("arbitrary", "arbitrary")),
        name="attn_combine_route",
    )(*os, *lses, h, w_out, expand, g, wr)


SHORT_BLOCK = 96


def _spread_tables(n_slots):
    n_cols = N_EXPERTS * n_slots
    col_expert = jnp.arange(n_cols, dtype=jnp.int32) // n_slots
    src = jnp.arange(LANES, dtype=jnp.int32)
    spread = (src[:, None] == col_expert[None, :] + N_EXPERTS).astype(BF16)
    slots = (jnp.arange(n_cols, dtype=jnp.int32) % n_slots).astype(F32).reshape(1, n_cols)
    return jnp.concatenate([spread, spread], axis=0), slots


def _slot_one_hot(route, spread_ref, slots_ref, weighted=False):
    tm = route.shape[0]
    lane = lax.broadcasted_iota(jnp.int32, route.shape, 1)
    sel = jnp.where((lane >= N_EXPERTS) & (lane < 2 * N_EXPERTS), route, 0.0)
    r = lax.broadcasted_iota(jnp.int32, (tm, tm), 0)
    c = lax.broadcasted_iota(jnp.int32, (tm, tm), 1)
    strict_lower = jnp.where(c < r, 1.0, 0.0).astype(BF16)
    rank = _dot(strict_lower, sel.astype(BF16))
    key = jnp.where(sel > 0.5, rank, -1.0).astype(BF16)
    hit = _dot(key, spread_ref[:LANES, :]) == slots_ref[...]
    if not weighted:
        return jnp.where(hit, 1.0, 0.0)
    w = pltpu.roll(route, N_EXPERTS, 1)
    w_hi = w.astype(BF16)
    w_lo = (w - w_hi.astype(F32)).astype(BF16)
    return jnp.where(hit, _dot(jnp.concatenate([w_hi, w_lo], axis=1), spread_ref[...]), 0.0)


def _block_copy(src, dst, sem):
    return pltpu.make_async_copy(src, dst, sem)


def _when_short(short_ref, step, fn, tables, tm):
    @pl.when(short_ref[step] > 0)
    def _():
        fn(SHORT_BLOCK, *tables[:2])

    @pl.when(short_ref[step] == 0)
    def _():
        fn(tm, *tables[2:])


def _dispatch_kernel(offs_ref, short_ref, fill_ref, route_ref, hn_ref, sp_s, sl_s, sp_l, sl_l, xs_hbm, stage, sem):
    tables = (sp_s, sl_s, sp_l, sl_l)
    j = pl.program_id(0)
    n = pl.num_programs(0)
    tm = hn_ref.shape[0]
    n_tiles = xs_hbm.shape[0] // tm
    slot = j % 2

    def zero_copy(row):
        return _block_copy(stage.at[1, 0], xs_hbm.at[pl.ds(pl.multiple_of(row, SUBLANES), tm)], sem.at[1])

    @pl.when(j == 0)
    def _():
        stage[1, 0] = jnp.zeros((tm, D_MODEL), F32)
        first_tail = fill_ref[N_EXPERTS]

        def start_tail(t, carry):
            zero_copy(t * tm).start()
            return carry

        def wait_tail(t, carry):
            zero_copy(t * tm).wait()
            return carry

        first = [zero_copy(fill_ref[e]) for e in range(N_EXPERTS)]
        second = [zero_copy(fill_ref[e] + tm) for e in range(N_EXPERTS)]
        for cp in first:
            cp.start()
        lax.fori_loop(first_tail, n_tiles, start_tail, 0)
        for cp in first:
            cp.wait()
        lax.fori_loop(first_tail, n_tiles, wait_tail, 0)
        for cp in second:
            cp.start()
        for cp in second:
            cp.wait()

    def copies(step_slot, step, n_rows):
        return [_block_copy(stage.at[step_slot, e, pl.ds(0, n_rows)],
                            xs_hbm.at[pl.ds(pl.multiple_of(offs_ref[step * N_EXPERTS + e], SUBLANES), n_rows)],
                            sem.at[step_slot])
                for e in range(N_EXPERTS)]

    def place(n_rows, spread_ref, slots_ref):
        one_hot = _slot_one_hot(route_ref[...], spread_ref, slots_ref).astype(BF16)
        placed = _dot_tn(one_hot, hn_ref[...])
        for e in range(N_EXPERTS):
            stage[slot, e, :n_rows] = placed[e * n_rows:(e + 1) * n_rows]

    _when_short(short_ref, j, place, tables, tm)

    @pl.when(j > 0)
    def _():
        _when_short(short_ref, j - 1, lambda n_rows, *_: [cp.wait() for cp in copies(1 - slot, j - 1, n_rows)],
                    tables, tm)

    _when_short(short_ref, j, lambda n_rows, *_: [cp.start() for cp in copies(slot, j, n_rows)], tables, tm)

    @pl.when(j == n - 1)
    def _():
        _when_short(short_ref, j, lambda n_rows, *_: [cp.wait() for cp in copies(slot, j, n_rows)], tables, tm)


def _table_specs(tm):
    return [_const_spec((2 * LANES, N_EXPERTS * SHORT_BLOCK)), _const_spec((1, N_EXPERTS * SHORT_BLOCK)),
            _const_spec((2 * LANES, N_EXPERTS * tm)), _const_spec((1, N_EXPERTS * tm))]


def _moe_dispatch(route, hn, offs, short, fill, tables, n_rows):
    T, D = hn.shape
    tm = TOKEN_TILE
    grid_spec = pltpu.PrefetchScalarGridSpec(
        num_scalar_prefetch=3,
        grid=(T // tm,),
        in_specs=[
            pl.BlockSpec((tm, LANES), lambda j, *_: (j, 0)),
            pl.BlockSpec((tm, D), lambda j, *_: (j, 0)),
        ] + _table_specs(tm),
        out_specs=pl.BlockSpec(memory_space=pl.ANY),
        scratch_shapes=[pltpu.VMEM((2, N_EXPERTS, tm, D), F32), pltpu.SemaphoreType.DMA((2,))],
    )
    return pl.pallas_call(
        _dispatch_kernel,
        grid_spec=grid_spec,
        out_shape=jax.ShapeDtypeStruct((n_rows, D), F32),
        compiler_params=_params(("arbitrary",)),
        name="moe_dispatch",
    )(offs, short, fill, route, hn, *tables)


EXPERT_F_CHUNK = 512


def _expert_kernel(te_ref, xb_ref, act_ref, first_ref, x_ref, wg_hbm, wu_hbm, wd_hbm, y_ref,
                   wg_buf, wu_buf, wd_buf, stg_gu, stg_d, sem):
    del xb_ref
    i = pl.program_id(0)
    tm = x_ref.shape[0]
    fc = EXPERT_F_CHUNK
    n_chunks = wg_buf.shape[1] // fc

    def chunk_copies(e, c, slot):
        cols = pl.ds(c * fc, fc)
        return [pltpu.make_async_copy(wg_hbm.at[e, :, cols], stg_gu.at[slot, 0], sem.at[slot]),
                pltpu.make_async_copy(wu_hbm.at[e, :, cols], stg_gu.at[slot, 1], sem.at[slot]),
                pltpu.make_async_copy(wd_hbm.at[e, cols, :], stg_d.at[slot], sem.at[slot])]

    def convert_chunk(c, slot):
        cs = slice(c * fc, (c + 1) * fc)
        wg_buf[:, cs] = stg_gu[slot, 0].astype(BF16)
        wu_buf[:, cs] = stg_gu[slot, 1].astype(BF16)
        wd_buf[cs, :] = stg_d[slot].astype(BF16)

    def swiglu_tile(load):
        x = x_ref[...].astype(BF16)
        acc = jnp.zeros((tm, D_MODEL), F32)
        if load is not None:
            for cp in chunk_copies(load, 0, 0):
                cp.start()
        for c in range(n_chunks):
            if load is not None:
                if c + 1 < n_chunks:
                    for cp in chunk_copies(load, c + 1, (c + 1) % 2):
                        cp.start()
                for cp in chunk_copies(load, c, c % 2):
                    cp.wait()
                convert_chunk(c, c % 2)
            cs = slice(c * fc, (c + 1) * fc)
            a = _silu(_dot(x, wg_buf[:, cs])) * _dot(x, wu_buf[:, cs])
            acc = acc + _dot(a.astype(BF16), wd_buf[cs, :])
        y_ref[...] = acc

    @pl.when((act_ref[i] > 0) & (first_ref[i] > 0))
    def _():
        swiglu_tile(te_ref[i])

    @pl.when((act_ref[i] > 0) & (first_ref[i] == 0))
    def _():
        swiglu_tile(None)

    @pl.when(act_ref[i] == 0)
    def _():
        y_ref[...] = jnp.zeros_like(y_ref)


def _moe_experts(xs, tile_expert, x_block, active, first, w_gate, w_up, w_down):
    n_rows, D = xs.shape
    Fd = w_gate.shape[2]
    tm = MOE_ROW_TILE
    fc = EXPERT_F_CHUNK
    G = n_rows // tm
    grid_spec = pltpu.PrefetchScalarGridSpec(
        num_scalar_prefetch=4,
        grid=(G,),
        in_specs=[
            pl.BlockSpec((tm, D), lambda i, te, xb, *_: (xb[i], 0)),
            pl.BlockSpec(memory_space=pl.ANY),
            pl.BlockSpec(memory_space=pl.ANY),
            pl.BlockSpec(memory_space=pl.ANY),
        ],
        out_specs=pl.BlockSpec((tm, D), lambda i, *_: (i, 0)),
        scratch_shapes=[
            pltpu.VMEM((D, Fd), BF16),
            pltpu.VMEM((D, Fd), BF16),
            pltpu.VMEM((Fd, D), BF16),
            pltpu.VMEM((2, 2, D, fc), F32),
            pltpu.VMEM((2, fc, D), F32),
            pltpu.SemaphoreType.DMA((2,)),
        ],
    )
    return pl.pallas_call(
        _expert_kernel,
        grid_spec=grid_spec,
        out_shape=jax.ShapeDtypeStruct((n_rows, D), F32),
        compiler_params=_params(("arbitrary",)),
        name="moe_experts",
    )(tile_expert, x_block, active, first, xs, w_gate, w_up, w_down)


def _finish_kernel(offs_ref, short_ref, route_ref, h_ref, g_ref, sp_s, sl_s, sp_l, sl_l, ys_hbm, out_ref, ybuf, sem):
    tables = (sp_s, sl_s, sp_l, sl_l)
    j = pl.program_id(0)
    n = pl.num_programs(0)
    tm = h_ref.shape[0]
    slot = j % 2

    def copies(step_slot, step, n_rows):
        return [_block_copy(ys_hbm.at[pl.ds(pl.multiple_of(offs_ref[step * N_EXPERTS + e], SUBLANES), n_rows)],
                            ybuf.at[step_slot, e, pl.ds(0, n_rows)], sem.at[step_slot])
                for e in range(N_EXPERTS)]

    @pl.when(j == 0)
    def _():
        _when_short(short_ref, 0, lambda n_rows, *_: [cp.start() for cp in copies(0, 0, n_rows)], tables, tm)

    @pl.when(j + 1 < n)
    def _():
        _when_short(short_ref, j + 1, lambda n_rows, *_: [cp.start() for cp in copies(1 - slot, j + 1, n_rows)],
                    tables, tm)

    def gather(n_rows, spread_ref, slots_ref):
        w = _slot_one_hot(route_ref[...], spread_ref, slots_ref, weighted=True)
        w_hi = w.astype(BF16)
        w_lo = (w - w_hi.astype(F32)).astype(BF16)
        for cp in copies(slot, j, n_rows):
            cp.wait()
        y = jnp.concatenate([ybuf[slot, e, :n_rows].astype(BF16) for e in range(N_EXPERTS)], axis=0)
        out_ref[...] = _rms(h_ref[...] + _dot(w_hi, y) + _dot(w_lo, y), g_ref[...])

    _when_short(short_ref, j, gather, tables, tm)


def _moe_finish(ys, offs, short, route, h, g, tables):
    T, D = h.shape
    tm = TOKEN_TILE
    grid_spec = pltpu.PrefetchScalarGridSpec(
        num_scalar_prefetch=2,
        grid=(T // tm,),
        in_specs=[
            pl.BlockSpec((tm, LANES), lambda j, *_: (j, 0)),
            pl.BlockSpec((tm, D), lambda j, *_: (j, 0)),
            pl.BlockSpec((1, D), lambda j, *_: (0, 0)),
        ] + _table_specs(tm) + [
            pl.BlockSpec(memory_space=pl.ANY),
        ],
        out_specs=pl.BlockSpec((tm, D), lambda j, *_: (j, 0)),
        scratch_shapes=[pltpu.VMEM((2, N_EXPERTS, tm, D), F32), pltpu.SemaphoreType.DMA((2,))],
    )
    return pl.pallas_call(
        _finish_kernel,
        grid_spec=grid_spec,
        out_shape=jax.ShapeDtypeStruct((T, D), F32),
        compiler_params=_params(("arbitrary",)),
        name="moe_finish",
    )(offs, short, route, h, g, *tables, ys)


def _route_metadata(route, tm):
    T = route.shape[0]
    nt = T // tm
    sel = (route[:, N_EXPERTS:2 * N_EXPERTS] > 0.5).astype(jnp.int32)
    cnt_tile = sel.reshape(nt, tm, N_EXPERTS).sum(axis=1)
    short = (jnp.max(cnt_tile, axis=1) <= SHORT_BLOCK).astype(jnp.int32)
    cnt_tile = ((cnt_tile + SUBLANES - 1) // SUBLANES) * SUBLANES
    n_e = cnt_tile.sum(axis=0)
    padded = ((n_e + tm - 1) // tm + 1) * tm
    ends = jnp.cumsum(padded)
    starts = ends - padded
    offs = starts[None, :] + jnp.cumsum(cnt_tile, axis=0) - cnt_tile
    fill = starts + n_e
    max_rows = TOP_K * T + nt * N_EXPERTS * (SUBLANES - 1)
    G = -(-max_rows // tm) + 2 * N_EXPERTS + 1
    tile_start = jnp.arange(G, dtype=jnp.int32) * tm
    tile_expert = jnp.minimum(jnp.sum((tile_start[:, None] >= ends[None, :]).astype(jnp.int32), axis=1),
                              N_EXPERTS - 1)
    active = (tile_start < fill[tile_expert]).astype(jnp.int32)
    x_block = jnp.where(active > 0, jnp.arange(G, dtype=jnp.int32), jnp.argmax(active).astype(jnp.int32))
    first = ((active > 0) & (tile_start == starts[tile_expert])).astype(jnp.int32)
    fill = jnp.concatenate([fill, ends[-1:] // tm])
    return (offs.reshape(-1).astype(jnp.int32), short, fill.astype(jnp.int32), tile_expert.astype(jnp.int32),
            x_block, active, first, G * tm)


def kernel(x, attn_norm, ffn_norm, hgrn_w_in, hgrn_lower_bounds, hgrn_out_norm, hgrn_w_out, kv_norm, w_kv,
           dil_w_q, dil_w_out, ffn_w_gate, ffn_w_up, ffn_w_down, moe_w_router, moe_w_gate, moe_w_up,
           moe_w_down, final_norm):
    B, S, D = x.shape
    row = lambda v: v.reshape(1, -1).astype(F32)

    lbs = jnp.cumsum(jax.nn.softmax(hgrn_lower_bounds.astype(F32), axis=0), axis=0)
    h = _hgrn_mixer(x, row(attn_norm[0]), hgrn_w_in[0].astype(BF16), row(lbs[0]), row(hgrn_out_norm[0]),
                    hgrn_w_out[0].astype(BF16))
    h = _dense_ffn(h, row(ffn_norm[0]), ffn_w_gate[0].astype(BF16), ffn_w_up[0].astype(BF16),
                   ffn_w_down[0].astype(BF16))

    cos_k, sin_k = _rope_tables(S, 1.0)
    cos_q, sin_q = _rope_tables(S, ATT_SCALE * LOG2E)
    w_k = _pair_block_columns(w_kv[:, :N_BRANCH * D])
    w_v = w_kv[:, N_BRANCH * D:].astype(BF16)
    w_q = _pair_block_columns(dil_w_q[0])
    ks, vs, qs = _norm_proj(h, row(kv_norm), row(attn_norm[1]), w_k, w_v, w_q, (cos_k, sin_k), (cos_q, sin_q))
    os, lses = [], []
    for i, dil in enumerate(DILATIONS):
        o_i, lse_i = _dilated_attn(qs[i], ks[i], vs[i], dil)
        os.append(o_i)
        lses.append(lse_i)

    head_of_col = jnp.arange(D, dtype=jnp.int32) // ATT_HEAD_DIM
    expand = (jnp.arange(LANES, dtype=jnp.int32)[:, None] == head_of_col[None, :]).astype(BF16)
    expand = jnp.concatenate([expand, expand], axis=0)
    wr = jnp.zeros((D, LANES), F32).at[:, :N_EXPERTS].set(moe_w_router[0].astype(F32))
    wr_hi = wr.astype(BF16)
    wr_lo = (wr - wr_hi.astype(F32)).astype(BF16)
    h, hn, route = _attn_combine(os, lses, h, dil_w_out[0].astype(BF16), expand, row(ffn_norm[1]),
                                 jnp.concatenate([wr_hi, wr_lo], axis=1))

    offs, short, fill, tile_expert, x_block, active, first, n_rows = _route_metadata(route, TOKEN_TILE)
    tables = _spread_tables(SHORT_BLOCK) + _spread_tables(TOKEN_TILE)
    xs = _moe_dispatch(route, hn, offs, short, fill, tables, n_rows)
    ys = _moe_experts(xs, tile_expert, x_block, active, first, moe_w_gate[0], moe_w_up[0], moe_w_down[0])
    out = _moe_finish(ys, offs, short, route, h, row(final_norm), tables)
    return out.reshape(B, S, D)
```
